```python
import jax, jax.numpy as jnp
from jax import lax
import numpy as np

D_MODEL = 2048
BATCH = 8
SEQ = 2048
DEPTH = 4

CHUNK = 64
N_MIXERS = 3
ATTN_HEADS = 32
ATTN_KV_HEADS = 4
ATTN_HEAD_DIM = D_MODEL // ATTN_HEADS
ATTN_GROUP = ATTN_HEADS // ATTN_KV_HEADS
WINDOW = 128
WINDOW_CHUNKS = WINDOW // CHUNK
ATTN_BLOCK = 128
CONV_WIDTH = 31
MLSTM_HEADS = 8
MLSTM_V_DIM = D_MODEL // MLSTM_HEADS
MLSTM_QK_DIM = MLSTM_V_DIM // 2
MLSTM_CHUNK = CHUNK
D_FF = 4 * D_MODEL
EPS = 1e-6
NEG = -1e30

kernel_name = "hybrid_swa_conformer_mlstm_trunk"


def rms_norm(x, g):
    xf = x.astype(jnp.float32)
    y = xf * lax.rsqrt(jnp.mean(jnp.square(xf), axis=-1, keepdims=True) + EPS)
    return (y * g.astype(jnp.float32)).astype(x.dtype)


def layer_norm(x, g, b):
    xf = x.astype(jnp.float32)
    mu = jnp.mean(xf, axis=-1, keepdims=True)
    var = jnp.mean(jnp.square(xf - mu), axis=-1, keepdims=True)
    y = (xf - mu) * lax.rsqrt(var + EPS) * g.astype(jnp.float32) + b.astype(jnp.float32)
    return y.astype(x.dtype)


def sliding_window_sink_attention(h, w_qkv, q_norm, k_norm, sinks, w_o):
    B, S, _ = h.shape
    H, KVH, G, Dh, QB = ATTN_HEADS, ATTN_KV_HEADS, ATTN_GROUP, ATTN_HEAD_DIM, ATTN_BLOCK
    nb = S // QB
    qkv = h @ w_qkv
    q, k, v = jnp.split(qkv, [H * Dh, H * Dh + KVH * Dh], axis=-1)
    q = rms_norm(q.reshape(B, S, KVH, G, Dh), q_norm)
    k = rms_norm(k.reshape(B, S, KVH, Dh), k_norm)
    v = v.reshape(B, S, KVH, Dh)
    qb = q.reshape(B, nb, QB, KVH, G, Dh)

    def band(t):
        tb = t.reshape(B, nb, QB, KVH, Dh)
        prev = jnp.pad(tb[:, :-1], ((0, 0), (1, 0), (0, 0), (0, 0), (0, 0)))
        return jnp.concatenate([prev, tb], axis=2)

    kb, vb = band(k), band(v)
    scores = jnp.einsum("bnqhgd,bnkhd->bnhgqk", qb, kb).astype(jnp.float32) * (Dh ** -0.5)
    blk = jnp.arange(nb)[:, None, None]
    q_pos = blk * QB + jnp.arange(QB)[None, :, None]
    k_pos = blk * QB - QB + jnp.arange(2 * QB)[None, None, :]
    q_chunk, k_chunk = q_pos // CHUNK, k_pos // CHUNK
    allowed = (k_pos >= 0) & (k_chunk <= q_chunk) & (k_chunk >= q_chunk - WINDOW_CHUNKS)
    scores = jnp.where(allowed[None, :, None, None, :, :], scores, NEG)
    s = sinks.astype(jnp.float32).reshape(KVH, G)[None, None, :, :, None, None]
    m = jnp.maximum(jnp.max(scores, axis=-1, keepdims=True), s)
    p = jnp.exp(scores - m)
    probs = p / (jnp.sum(p, axis=-1, keepdims=True) + jnp.exp(s - m))
    out = jnp.einsum("bnhgqk,bnkhd->bnqhgd", probs.astype(vb.dtype), vb)
    return out.reshape(B, S, H * Dh) @ w_o


def conformer_conv_module(h, w_in, b_in, dw, dw_b, ln_g, ln_b, w_out, b_out):
    u = h @ w_in + b_in
    a, gate = jnp.split(u, 2, axis=-1)
    u = a * jax.nn.sigmoid(gate)
    u = lax.conv_general_dilated(
        u, dw[:, None, :], window_strides=(1,), padding=[(CONV_WIDTH - 1, 0)],
        dimension_numbers=("NWC", "WIO", "NWC"), feature_group_count=D_MODEL) + dw_b
    u = jax.nn.silu(layer_norm(u, ln_g, ln_b))
    return u @ w_out + b_out


def mlstm_chunkwise(h, w_in, b_gates, h_norm, w_out):
    B, S, _ = h.shape
    H, DK, DV, L = MLSTM_HEADS, MLSTM_QK_DIM, MLSTM_V_DIM, MLSTM_CHUNK
    nc = S // L
    f32 = jnp.float32
    proj = h @ w_in
    q, k, v, o_pre, gates_pre = jnp.split(
        proj, [H * DK, 2 * H * DK, 2 * H * DK + D_MODEL, 2 * H * DK + 2 * D_MODEL], axis=-1)
    o_gate = jax.nn.sigmoid(o_pre + b_gates[:D_MODEL])
    gates = (gates_pre + b_gates[D_MODEL:]).astype(f32)
    i_pre, f_pre = gates[..., :H], gates[..., H:]
    logf = jax.nn.log_sigmoid(f_pre)

    def heads(t, d):
        return t.reshape(B, nc, L, H, d).transpose(0, 3, 1, 2, 4).astype(f32)

    def gate_layout(t):
        return t.reshape(B, nc, L, H).transpose(0, 3, 1, 2)

    q = heads(q, DK) * (DK ** -0.5)
    k, v = heads(k, DK), heads(v, DV)
    i_c, lf_c = gate_layout(i_pre), gate_layout(logf)
    b = jnp.cumsum(lf_c, axis=-1)
    g = b[..., -1]
    a = g[..., None] - b + i_c

    def step(carry, inp):
        C, n, m = carry
        k_c, v_c, a_c, g_c = inp
        m_new = jnp.maximum(g_c + m, jnp.max(a_c, axis=-1))
        decay = jnp.exp(g_c + m - m_new)
        kw = k_c * jnp.exp(a_c - m_new[..., None])[..., None]
        C_new = decay[..., None, None] * C + jnp.einsum("bhlk,bhlv->bhkv", kw, v_c)
        n_new = decay[..., None] * n + jnp.sum(kw, axis=-2)
        return (C_new, n_new, m_new), (C, n, m)

    init = (jnp.zeros((B, H, DK, DV), f32), jnp.zeros((B, H, DK), f32), jnp.zeros((B, H), f32))
    xs = (jnp.moveaxis(k, 2, 0), jnp.moveaxis(v, 2, 0), jnp.moveaxis(a, 2, 0), jnp.moveaxis(g, 2, 0))
    _, (C_prev, n_prev, m_prev) = lax.scan(step, init, xs)
    C_prev = jnp.moveaxis(C_prev, 0, 2)
    n_prev = jnp.moveaxis(n_prev, 0, 2)
    m_prev = jnp.moveaxis(m_prev, 0, 2)

    causal = jnp.tril(jnp.ones((L, L), dtype=bool))
    dmat = jnp.where(causal, b[..., :, None] - b[..., None, :] + i_c[..., None, :], NEG)
    inter_log = b + m_prev[..., None]
    m_row = jnp.maximum(inter_log, jnp.max(dmat, axis=-1))
    w_intra = jnp.exp(dmat - m_row[..., None])
    w_inter = jnp.exp(inter_log - m_row)
    qk = jnp.einsum("bhnld,bhnsd->bhnls", q, k) * w_intra
    num = jnp.einsum("bhnls,bhnsv->bhnlv", qk, v) + \
        w_inter[..., None] * jnp.einsum("bhnld,bhndv->bhnlv", q, C_prev)
    qn = jnp.sum(qk, axis=-1) + w_inter * jnp.einsum("bhnld,bhnd->bhnl", q, n_prev)
    h_tilde = num / jnp.maximum(jnp.abs(qn), jnp.exp(-m_row))[..., None]
    h_tilde = h_tilde.transpose(0, 2, 3, 1, 4).reshape(B, S, H, DV)
    h_tilde = rms_norm(h_tilde, h_norm.reshape(H, DV)).reshape(B, S, D_MODEL).astype(h.dtype)
    return (o_gate * h_tilde) @ w_out


def squared_relu_mlp(h, w1, w2):
    return jnp.square(jax.nn.relu(h @ w1)) @ w2


def _normal(key, shape, scale):
    return jax.random.normal(key, shape, jnp.float32) * scale


def _gain(key, shape):
    return 1.0 + 0.02 * jax.random.normal(key, shape, jnp.float32)


def _attn_params(key, p):
    ks = jax.random.split(key, 5)
    qkv_cols = (ATTN_HEADS + 2 * ATTN_KV_HEADS) * ATTN_HEAD_DIM
    return {
        p + "attn_w_qkv": _normal(ks[0], (D_MODEL, qkv_cols), D_MODEL ** -0.5),
        p + "attn_q_norm": _gain(ks[1], (ATTN_HEAD_DIM,)),
        p + "attn_k_norm": _gain(ks[2], (ATTN_HEAD_DIM,)),
        p + "attn_sinks": _normal(ks[3], (ATTN_HEADS,), 0.5),
        p + "attn_w_o": _normal(ks[4], (ATTN_HEADS * ATTN_HEAD_DIM, D_MODEL), (ATTN_HEADS * ATTN_HEAD_DIM) ** -0.5),
    }


def _conv_params(key, p):
    ks = jax.random.split(key, 8)
    return {
        p + "conv_w_in": _normal(ks[0], (D_MODEL, 2 * D_MODEL), D_MODEL ** -0.5),
        p + "conv_b_in": _normal(ks[1], (2 * D_MODEL,), 0.01),
        p + "conv_dw": _normal(ks[2], (CONV_WIDTH, D_MODEL), CONV_WIDTH ** -0.5),
        p + "conv_dw_b": _normal(ks[3], (D_MODEL,), 0.01),
        p + "conv_ln_g": _gain(ks[4], (D_MODEL,)),
        p + "conv_ln_b": _normal(ks[5], (D_MODEL,), 0.01),
        p + "conv_w_out": _normal(ks[6], (D_MODEL, D_MODEL), D_MODEL ** -0.5),
        p + "conv_b_out": _normal(ks[7], (D_MODEL,), 0.01),
    }


def _mlstm_params(key, p):
    ks = jax.random.split(key, 5)
    in_cols = 2 * MLSTM_HEADS * MLSTM_QK_DIM + 2 * D_MODEL + 2 * MLSTM_HEADS
    b_gates = jnp.concatenate([
        _normal(ks[1], (D_MODEL + MLSTM_HEADS,), 0.01),
        3.0 + _normal(ks[2], (MLSTM_HEADS,), 0.5),
    ])
    return {
        p + "mlstm_w_in": _normal(ks[0], (D_MODEL, in_cols), D_MODEL ** -0.5),
        p + "mlstm_b_gates": b_gates,
        p + "mlstm_h_norm": _gain(ks[3], (D_MODEL,)),
        p + "mlstm_w_out": _normal(ks[4], (D_MODEL, D_MODEL), D_MODEL ** -0.5),
    }


def setup_inputs(seed: int = 0) -> dict:
    key = jax.random.key(seed)
    keys = jax.random.split(key, DEPTH + 1)
    makers = (_attn_params, _conv_params, _mlstm_params)
    params = {"x": jax.random.normal(keys[0], (BATCH, SEQ, D_MODEL), jnp.float32)}
    for i in range(DEPTH):
        lk = jax.random.split(keys[i + 1], 5)
        p = f"l{i}_"
        params[p + "mix_norm"] = _gain(lk[0], (D_MODEL,))
        params.update(makers[i % N_MIXERS](lk[1], p))
        params[p + "mlp_norm"] = _gain(lk[2], (D_MODEL,))
        params[p + "mlp_w1"] = _normal(lk[3], (D_MODEL, D_FF), D_MODEL ** -0.5)
        params[p + "mlp_w2"] = _normal(lk[4], (D_FF, D_MODEL), D_FF ** -0.5)
    return params


def reference(x,
              l0_mix_norm, l0_attn_w_qkv, l0_attn_q_norm, l0_attn_k_norm, l0_attn_sinks, l0_attn_w_o,
              l0_mlp_norm, l0_mlp_w1, l0_mlp_w2,
              l1_mix_norm, l1_conv_w_in, l1_conv_b_in, l1_conv_dw, l1_conv_dw_b, l1_conv_ln_g, l1_conv_ln_b,
              l1_conv_w_out, l1_conv_b_out,
              l1_mlp_norm, l1_mlp_w1, l1_mlp_w2,
              l2_mix_norm, l2_mlstm_w_in, l2_mlstm_b_gates, l2_mlstm_h_norm, l2_mlstm_w_out,
              l2_mlp_norm, l2_mlp_w1, l2_mlp_w2,
              l3_mix_norm, l3_attn_w_qkv, l3_attn_q_norm, l3_attn_k_norm, l3_attn_sinks, l3_attn_w_o,
              l3_mlp_norm, l3_mlp_w1, l3_mlp_w2):
    mixer_fns = (sliding_window_sink_attention, conformer_conv_module, mlstm_chunkwise)
    mix_norms = (l0_mix_norm, l1_mix_norm, l2_mix_norm, l3_mix_norm)
    mixer_args = (
        (l0_attn_w_qkv, l0_attn_q_norm, l0_attn_k_norm, l0_attn_sinks, l0_attn_w_o),
        (l1_conv_w_in, l1_conv_b_in, l1_conv_dw, l1_conv_dw_b, l1_conv_ln_g, l1_conv_ln_b,
         l1_conv_w_out, l1_conv_b_out),
        (l2_mlstm_w_in, l2_mlstm_b_gates, l2_mlstm_h_norm, l2_mlstm_w_out),
        (l3_attn_w_qkv, l3_attn_q_norm, l3_attn_k_norm, l3_attn_sinks, l3_attn_w_o),
    )
    mlp_norms = (l0_mlp_norm, l1_mlp_norm, l2_mlp_norm, l3_mlp_norm)
    mlp_w1s = (l0_mlp_w1, l1_mlp_w1, l2_mlp_w1, l3_mlp_w1)
    mlp_w2s = (l0_mlp_w2, l1_mlp_w2, l2_mlp_w2, l3_mlp_w2)
    for i in range(DEPTH):
        x = x + mixer_fns[i % N_MIXERS](rms_norm(x, mix_norms[i]), *mixer_args[i])
        x = x + squared_relu_mlp(rms_norm(x, mlp_norms[i]), mlp_w1s[i], mlp_w2s[i])
    return x
```

```python
import functools

import jax
import jax.numpy as jnp
from jax import lax
from jax.experimental import pallas as pl
from jax.experimental.pallas import tpu as pltpu

F32 = jnp.float32
BF16 = jnp.bfloat16

EPS = 1e-6
NEG = -1e30

ATTN_HEADS = 32
ATTN_KV_HEADS = 4
ATTN_GROUP = ATTN_HEADS // ATTN_KV_HEADS
ATTN_HEAD_DIM = 64
ATTN_BLOCK = 128
CHUNK = 64
CONV_WIDTH = 31
MLSTM_HEADS = 8
MLSTM_QK_DIM = 128
MLSTM_V_DIM = 256
MLSTM_CHUNK = 128

LANES = 128
VMEM_LIMIT_BYTES = 60000 * 1024

CONV_HALO = 32


def _params(*sem):
    return pltpu.CompilerParams(dimension_semantics=sem, vmem_limit_bytes=VMEM_LIMIT_BYTES)


def _resident(shape):
    nd = len(shape)
    return pl.BlockSpec(shape, lambda *_: (0,) * nd, pipeline_mode=pl.Buffered(1))


def _rms_norm_rows(x, gain):
    return x * lax.rsqrt(jnp.mean(x * x, axis=-1, keepdims=True) + EPS) * gain


def _mlp_body(x_ref, g_ref, w1_ref, w2_ref, o_ref, h_ref):
    @pl.when(pl.program_id(1) == 0)
    def _():
        x = x_ref[...]
        h_ref[...] = _rms_norm_rows(x, g_ref[...]).astype(BF16)
        o_ref[...] = x

    a = jnp.dot(h_ref[...], w1_ref[...], preferred_element_type=F32)
    a = jnp.square(jnp.maximum(a, 0.0)).astype(BF16)
    o_ref[...] += jnp.dot(a, w2_ref[...], preferred_element_type=F32)


def _mlp(x2d, gain, w1, w2, *, bm=512, fc=1024):
    m, d = x2d.shape
    ff = w1.shape[1]
    return pl.pallas_call(
        _mlp_body,
        out_shape=jax.ShapeDtypeStruct((m, d), F32),
        grid=(m // bm, ff // fc),
        in_specs=[
            pl.BlockSpec((bm, d), lambda i, j: (i, 0)),
            pl.BlockSpec((1, d), lambda i, j: (0, 0)),
            pl.BlockSpec((d, fc), lambda i, j: (0, j)),
            pl.BlockSpec((fc, d), lambda i, j: (j, 0)),
        ],
        out_specs=pl.BlockSpec((bm, d), lambda i, j: (i, 0)),
        scratch_shapes=[pltpu.VMEM((bm, d), BF16)],
        compiler_params=_params("parallel", "arbitrary"),
        name="mlp",
    )(x2d, gain.reshape(1, d), w1, w2)


def _proj_res_body(a_ref, w_ref, b_ref, x_ref, o_ref):
    acc = jnp.dot(a_ref[...], w_ref[...], preferred_element_type=F32)
    o_ref[...] = x_ref[...] + acc + b_ref[...]


def _proj_res(a2d, w, bias, x2d, *, bm=512):
    m, k = a2d.shape
    d = w.shape[1]
    return pl.pallas_call(
        _proj_res_body,
        out_shape=jax.ShapeDtypeStruct((m, d), F32),
        grid=(m // bm,),
        in_specs=[
            pl.BlockSpec((bm, k), lambda i: (i, 0)),
            _resident((k, d)),
            _resident((1, d)),
            pl.BlockSpec((bm, d), lambda i: (i, 0)),
        ],
        out_specs=pl.BlockSpec((bm, d), lambda i: (i, 0)),
        compiler_params=_params("parallel"),
        name="proj_res",
    )(a2d, w, bias.reshape(1, d), x2d)


def _attn_qkv_body(x_ref, g_ref, wqvT_ref, wk_ref, kgain_ref, qT_ref, k_ref, vT_ref):
    h = _rms_norm_rows(x_ref[...], g_ref[...]).astype(BF16)
    nq = ATTN_HEADS * ATTN_HEAD_DIM
    qvT = lax.dot_general(wqvT_ref[...], h, (((1,), (1,)), ((), ())),
                          preferred_element_type=F32)
    bm = h.shape[0]
    q3 = qvT[:nq].reshape(ATTN_HEADS, ATTN_HEAD_DIM, bm)
    inv = lax.rsqrt(jnp.mean(q3 * q3, axis=1, keepdims=True) + EPS) * (ATTN_HEAD_DIM ** -0.5)
    qT_ref[...] = (q3 * inv).reshape(nq, bm).astype(BF16)
    vT_ref[...] = qvT[nq:].astype(BF16)
    kf = jnp.dot(h, wk_ref[...], preferred_element_type=F32)
    for j in range(ATTN_KV_HEADS):
        kj = kf[:, j * LANES:j * LANES + ATTN_HEAD_DIM]
        kn = kj * lax.rsqrt(jnp.mean(kj * kj, axis=-1, keepdims=True) + EPS)
        k_ref[j] = (kn * kgain_ref[...]).astype(BF16)


def _attn_qkv(x3d, gain, wqvT, wk_pad, kgain, *, bm=512):
    b, s, d = x3d.shape
    nq = ATTN_HEADS * ATTN_HEAD_DIM
    nkv = ATTN_KV_HEADS * ATTN_HEAD_DIM
    return pl.pallas_call(
        _attn_qkv_body,
        out_shape=(
            jax.ShapeDtypeStruct((b, nq, s), BF16),
            jax.ShapeDtypeStruct((b, ATTN_KV_HEADS, s, ATTN_HEAD_DIM), BF16),
            jax.ShapeDtypeStruct((b, nkv, s), BF16),
        ),
        grid=(b, s // bm),
        in_specs=[
            pl.BlockSpec((None, bm, d), lambda i, j: (i, j, 0)),
            _resident((1, d)),
            _resident((nq + nkv, d)),
            _resident((d, ATTN_KV_HEADS * LANES)),
            _resident((1, ATTN_HEAD_DIM)),
        ],
        out_specs=(
            pl.BlockSpec((None, nq, bm), lambda i, j: (i, 0, j)),
            pl.BlockSpec((None, ATTN_KV_HEADS, bm, ATTN_HEAD_DIM), lambda i, j: (i, 0, j, 0)),
            pl.BlockSpec((None, nkv, bm), lambda i, j: (i, 0, j)),
        ),
        compiler_params=_params("parallel", "parallel"),
        name="attn_qkv",
    )(x3d, gain.reshape(1, d), wqvT, wk_pad, kgain.reshape(1, ATTN_HEAD_DIM))


def _attn_core_body(qT_ref, kp_ref, ko_ref, vTp_ref, vTo_ref, sink_ref, oT_ref):
    n = pl.program_id(1)
    qb = ATTN_BLOCK
    width = ATTN_GROUP * qb
    row = lax.broadcasted_iota(jnp.int32, (2 * qb, width), 0)
    col = lax.broadcasted_iota(jnp.int32, (2 * qb, width), 1)
    kc = row // CHUNK
    qc = (col % qb) // CHUNK
    allowed = (kc >= qc) & (kc <= qc + 2) & ((row >= qb) | (n > 0))
    hd = ATTN_HEAD_DIM
    for h in range(ATTN_KV_HEADS):
        kb = jnp.concatenate([kp_ref[h], ko_ref[h]], axis=0)
        q = jnp.concatenate(
            [qT_ref[(h * ATTN_GROUP + g) * hd:(h * ATTN_GROUP + g + 1) * hd, :] for g in range(ATTN_GROUP)],
            axis=1)
        s = jnp.dot(kb, q, preferred_element_type=F32)
        s = jnp.where(allowed, s, NEG)
        sk = sink_ref[h]
        m = jnp.maximum(jnp.max(s, axis=0, keepdims=True), sk)
        p = jnp.exp(s - m)
        l = jnp.sum(p, axis=0, keepdims=True) + jnp.exp(sk - m)
        vb = jnp.concatenate([vTp_ref[h * hd:(h + 1) * hd, :], vTo_ref[h * hd:(h + 1) * hd, :]], axis=1)
        o = jnp.dot(vb, p.astype(BF16), preferred_element_type=F32) / l
        for g in range(ATTN_GROUP):
            r0 = (h * ATTN_GROUP + g) * hd
            oT_ref[r0:r0 + hd, :] = o[:, g * qb:(g + 1) * qb].astype(BF16)


def _attn_core(qT, k, vT, sink_rows):
    b, nq, s = qT.shape
    nkv = vT.shape[1]
    qb = ATTN_BLOCK
    prev = lambda j: jnp.maximum(j - 1, 0)
    return pl.pallas_call(
        _attn_core_body,
        out_shape=jax.ShapeDtypeStruct((b, nq, s), BF16),
        grid=(b, s // qb),
        in_specs=[
            pl.BlockSpec((None, nq, qb), lambda i, j: (i, 0, j)),
            pl.BlockSpec((None, ATTN_KV_HEADS, qb, ATTN_HEAD_DIM), lambda i, j: (i, 0, prev(j), 0)),
            pl.BlockSpec((None, ATTN_KV_HEADS, qb, ATTN_HEAD_DIM), lambda i, j: (i, 0, j, 0)),
            pl.BlockSpec((None, nkv, qb), lambda i, j: (i, 0, prev(j))),
            pl.BlockSpec((None, nkv, qb), lambda i, j: (i, 0, j)),
            _resident((ATTN_KV_HEADS, 1, ATTN_GROUP * qb)),
        ],
        out_specs=pl.BlockSpec((None, nq, qb), lambda i, j: (i, 0, j)),
        compiler_params=_params("parallel", "parallel"),
        name="attn_core",
    )(qT, k, k, vT, vT, sink_rows)


def _proj_res_t_body(aT_ref, w_ref, x_ref, o_ref):
    acc = lax.dot_general(aT_ref[...], w_ref[...], (((0,), (0,)), ((), ())), preferred_element_type=F32)
    o_ref[...] = x_ref[...] + acc


def _proj_res_t(aT, w, x3d, *, bm=512):
    b, k, s = aT.shape
    d = w.shape[1]
    return pl.pallas_call(
        _proj_res_t_body,
        out_shape=jax.ShapeDtypeStruct((b, s, d), F32),
        grid=(b, s // bm),
        in_specs=[
            pl.BlockSpec((None, k, bm), lambda i, j: (i, 0, j)),
            _resident((k, d)),
            pl.BlockSpec((None, bm, d), lambda i, j: (i, j, 0)),
        ],
        out_specs=pl.BlockSpec((None, bm, d), lambda i, j: (i, j, 0)),
        compiler_params=_params("parallel", "parallel"),
        name="proj_res_t",
    )(aT, w, x3d)


def _attention_layer(x3d, mix_norm, w_qkv, q_norm, k_norm, sinks, w_o):
    b, s, d = x3d.shape
    nq = ATTN_HEADS * ATTN_HEAD_DIM
    nkv = ATTN_KV_HEADS * ATTN_HEAD_DIM
    wq, wk, wv = w_qkv[:, :nq], w_qkv[:, nq:nq + nkv], w_qkv[:, nq + nkv:]
    wqvT = jnp.concatenate([wq, wv], axis=1).T.astype(BF16)
    wk_pad = jnp.pad(wk.reshape(d, ATTN_KV_HEADS, ATTN_HEAD_DIM),
                     ((0, 0), (0, 0), (0, LANES - ATTN_HEAD_DIM))).reshape(d, ATTN_KV_HEADS * LANES).astype(BF16)
    kgain = (q_norm * k_norm).astype(F32)
    sink_rows = jnp.repeat(sinks.astype(F32).reshape(ATTN_KV_HEADS, 1, ATTN_GROUP), ATTN_BLOCK, axis=2)
    qT, k, vT = _attn_qkv(x3d, mix_norm, wqvT, wk_pad, kgain)
    oT = _attn_core(qT, k, vT, sink_rows)
    return _proj_res_t(oT, w_o.astype(BF16), x3d)


def _conv_body(x_ref, g_ref, win_ref, bin_ref, dw_ref, dwb_ref, lng_ref, lnb_ref, wout_ref, bout_ref,
               o_ref, gbuf, cbuf, ybuf):
    bm, d = x_ref.shape
    nslab = d // LANES

    @pl.when(pl.program_id(1) == 0)
    def _():
        gbuf[:, 0:CONV_HALO, :] = jnp.zeros((nslab, CONV_HALO, LANES), F32)

    x = x_ref[...]
    h = _rms_norm_rows(x, g_ref[...]).astype(BF16)
    u = jnp.dot(h, win_ref[...], preferred_element_type=F32) + bin_ref[...]
    glu = u[:, :d] * jax.nn.sigmoid(u[:, d:])
    for j in range(nslab):
        gbuf[j, CONV_HALO:CONV_HALO + bm, :] = glu[:, j * LANES:(j + 1) * LANES]

    def slab(j, carry):
        acc = jnp.broadcast_to(dwb_ref[j], (bm, LANES))
        taps = dw_ref[j]
        for t in range(CONV_WIDTH):
            r0 = CONV_HALO - (CONV_WIDTH - 1) + t
            acc = acc + taps[t:t + 1, :] * gbuf[j, pl.ds(r0, bm), :]
        cbuf[j] = acc
        gbuf[j, 0:CONV_HALO, :] = gbuf[j, pl.ds(bm, CONV_HALO), :]
        return carry

    lax.fori_loop(0, nslab, slab, 0)

    tot = cbuf[0]
    for j in range(1, nslab):
        tot = tot + cbuf[j]
    mu = jnp.sum(tot, axis=-1, keepdims=True) * (1.0 / d)
    sq = None
    for j in range(nslab):
        c = cbuf[j] - mu
        sq = c * c if sq is None else sq + c * c
    inv = lax.rsqrt(jnp.sum(sq, axis=-1, keepdims=True) * (1.0 / d) + EPS)
    for j in range(nslab):
        y = (cbuf[j] - mu) * inv * lng_ref[:, j * LANES:(j + 1) * LANES] + lnb_ref[:, j * LANES:(j + 1) * LANES]
        ybuf[:, j * LANES:(j + 1) * LANES] = (y * jax.nn.sigmoid(y)).astype(BF16)
    o_ref[...] = x + jnp.dot(ybuf[...], wout_ref[...], preferred_element_type=F32) + bout_ref[...]


def _conv_layer(x3d, mix_norm, w_in, b_in, dw, dw_b, ln_g, ln_b, w_out, b_out, *, bm=256):
    b, s, d = x3d.shape
    nslab = d // LANES
    dw_slabs = dw.astype(F32).reshape(CONV_WIDTH, nslab, LANES).transpose(1, 0, 2)
    dwb_slabs = dw_b.astype(F32).reshape(nslab, 1, LANES)
    row = lambda v: v.astype(F32).reshape(1, -1)
    return pl.pallas_call(
        _conv_body,
        out_shape=jax.ShapeDtypeStruct((b, s, d), F32),
        grid=(b, s // bm),
        in_specs=[
            pl.BlockSpec((None, bm, d), lambda i, j: (i, j, 0)),
            _resident((1, d)),
            _resident((d, 2 * d)),
            _resident((1, 2 * d)),
            _resident((nslab, CONV_WIDTH, LANES)),
            _resident((nslab, 1, LANES)),
            _resident((1, d)),
            _resident((1, d)),
            _resident((d, d)),
            _resident((1, d)),
        ],
        out_specs=pl.BlockSpec((None, bm, d), lambda i, j: (i, j, 0)),
        scratch_shapes=[
            pltpu.VMEM((nslab, CONV_HALO + bm, LANES), F32),
            pltpu.VMEM((nslab, bm, LANES), F32),
            pltpu.VMEM((bm, d), BF16),
        ],
        compiler_params=_params("parallel", "arbitrary"),
        name="conv_module",
    )(x3d, row(mix_norm), w_in.astype(BF16), row(b_in), dw_slabs, dwb_slabs, row(ln_g), row(ln_b),
      w_out.astype(BF16), row(b_out))


def _mlstm_proj_body(x_ref, g_ref, wq_ref, wkT_ref, wv_ref, wo_ref, wi_ref, wf_ref, bo_ref, bi_ref, bf_ref,
                     q_ref, kT_ref, v_ref, og_ref, gi_ref, gf_ref):
    h = _rms_norm_rows(x_ref[...], g_ref[...]).astype(BF16)
    q = jnp.dot(h, wq_ref[...], preferred_element_type=F32)
    q_ref[...] = (q * (MLSTM_QK_DIM ** -0.5)).astype(BF16)
    kT_ref[...] = lax.dot_general(wkT_ref[...], h, (((1,), (1,)), ((), ())),
                                  preferred_element_type=F32).astype(BF16)
    v_ref[...] = jnp.dot(h, wv_ref[...], preferred_element_type=F32).astype(BF16)
    o_pre = jnp.dot(h, wo_ref[...], preferred_element_type=F32) + bo_ref[...]
    og_ref[...] = jax.nn.sigmoid(o_pre).astype(BF16)
    gi_ref[...] = jnp.dot(h, wi_ref[...], preferred_element_type=F32) + bi_ref[...]
    gf_ref[...] = jnp.dot(h, wf_ref[...], preferred_element_type=F32) + bf_ref[...]


def _mlstm_proj(x3d, gain, wq, wkT, wv, wo, wi, wf, bo, bi, bf, *, bm=256):
    b, s, d = x3d.shape
    nqk = MLSTM_HEADS * MLSTM_QK_DIM
    tok = lambda n: pl.BlockSpec((None, bm, n), lambda i, j: (i, j, 0))
    return pl.pallas_call(
        _mlstm_proj_body,
        out_shape=(
            jax.ShapeDtypeStruct((b, s, nqk), BF16),
            jax.ShapeDtypeStruct((b, nqk, s), BF16),
            jax.ShapeDtypeStruct((b, s, d), BF16),
            jax.ShapeDtypeStruct((b, s, d), BF16),
            jax.ShapeDtypeStruct((b, s, LANES), F32),
            jax.ShapeDtypeStruct((b, s, LANES), F32),
        ),
        grid=(b, s // bm),
        in_specs=[
            tok(d),
            _resident((1, d)),
            _resident((d, nqk)),
            _resident((nqk, d)),
            _resident((d, d)),
            _resident((d, d)),
            _resident((d, LANES)),
            _resident((d, LANES)),
            _resident((1, d)),
            _resident((1, LANES)),
            _resident((1, LANES)),
        ],
        out_specs=(
            tok(nqk),
            pl.BlockSpec((None, nqk, bm), lambda i, j: (i, 0, j)),
            tok(d),
            tok(d),
            tok(LANES),
            tok(LANES),
        ),
        compiler_params=_params("parallel", "parallel"),
        name="mlstm_proj",
    )(x3d, gain.reshape(1, d), wq, wkT, wv, wo, wi, wf, bo, bi, bf)


def _scan_rows(x, op, fill, period):
    r = lax.broadcasted_iota(jnp.int32, x.shape, 0) % period
    k = 1
    while k < period:
        x = op(x, jnp.where(r >= k, pltpu.roll(x, k, axis=0), fill))
        k *= 2
    return x


def _mlstm_gates_body(gi_ref, gf_ref, negm_ref, winter_ref, floor_ref, u_ref, kw_ref, dec_ref):
    s = gi_ref.shape[0]
    L = MLSTM_CHUNK
    nc = s // L
    gf = gf_ref[...]
    logf = jnp.minimum(gf, 0.0) - jnp.log(1.0 + jnp.exp(-jnp.abs(gf)))
    b = _scan_rows(logf, jnp.add, 0.0, L)
    u = gi_ref[...] - b
    cm = _scan_rows(u, jnp.maximum, -jnp.inf, L)
    m = jnp.zeros((1, LANES), F32)
    for c in range(nc):
        rows = slice(c * L, (c + 1) * L)
        cm_c = cm[rows]
        b_c = b[rows]
        u_c = u[rows]
        big_m = jnp.maximum(m, cm_c)
        m_last = big_m[L - 1:L, :]
        negm = -big_m
        wint = jnp.exp(m - big_m)
        flo = jnp.exp(-(b_c + big_m))
        for hh in range(MLSTM_HEADS):
            negm_ref[hh, rows, :] = jnp.broadcast_to(negm[:, hh:hh + 1], (L, LANES))
            winter_ref[hh, rows, :] = jnp.broadcast_to(wint[:, hh:hh + 1], (L, LANES))
            floor_ref[hh, rows, :] = jnp.broadcast_to(flo[:, hh:hh + 1], (L, LANES))
        uT = u_c.T
        mlT = jnp.broadcast_to(m_last, (L, LANES)).T
        mpT = jnp.broadcast_to(m, (L, LANES)).T
        u_ref[:, rows] = uT[:MLSTM_HEADS]
        kw_ref[:, rows] = jnp.exp(uT - mlT)[:MLSTM_HEADS]
        dec_ref[:, rows] = jnp.exp(mpT - mlT)[:MLSTM_HEADS]
        m = b_c[L - 1:L, :] + m_last


def _mlstm_gates(gi, gf):
    b, s, _ = gi.shape
    hcol = jax.ShapeDtypeStruct((b, MLSTM_HEADS, s, LANES), F32)
    hrow = jax.ShapeDtypeStruct((b, MLSTM_HEADS, s), F32)
    tok = pl.BlockSpec((None, s, LANES), lambda i: (i, 0, 0))
    col = pl.BlockSpec((None, MLSTM_HEADS, s, LANES), lambda i: (i, 0, 0, 0))
    rowspec = pl.BlockSpec((None, MLSTM_HEADS, s), lambda i: (i, 0, 0))
    return pl.pallas_call(
        _mlstm_gates_body,
        out_shape=(hcol, hcol, hcol, hrow, hrow, hrow),
        grid=(b,),
        in_specs=[tok, tok],
        out_specs=(col, col, col, rowspec, rowspec, rowspec),
        compiler_params=_params("parallel"),
        name="mlstm_gates",
    )(gi, gf)


def _mlstm_core_body(q_ref, kT_ref, v_ref, og_ref, hn_ref, negm_ref, winter_ref, floor_ref,
                     u_ref, kw_ref, dec_ref, o_ref):
    L = MLSTM_CHUNK
    dk, dv = MLSTM_QK_DIM, MLSTM_V_DIM
    nc = q_ref.shape[0] // L
    tril = lax.broadcasted_iota(jnp.int32, (L, L), 0) >= lax.broadcasted_iota(jnp.int32, (L, L), 1)

    def chunk(c, carry):
        cmat, nvec = carry
        r0 = pl.multiple_of(c * L, L)
        q = q_ref[pl.ds(r0, L), :]
        kT = kT_ref[:, pl.ds(r0, L)]
        v = v_ref[pl.ds(r0, L), :]
        u = u_ref[:, pl.ds(r0, L)]
        kw = kw_ref[:, pl.ds(r0, L)]
        dec = dec_ref[:, pl.ds(r0, L)]
        negm = negm_ref[pl.ds(r0, L), :]
        wint = winter_ref[pl.ds(r0, L), :]
        flo = floor_ref[pl.ds(r0, L), :]

        dmat = jnp.where(tril, jnp.exp(u + negm), 0.0)
        p = jnp.dot(q, kT, preferred_element_type=F32) * dmat
        c_prev = cmat.astype(BF16)
        num = jnp.dot(p.astype(BF16), v, preferred_element_type=F32)
        num = num + jnp.concatenate([wint, wint], axis=1) * jnp.dot(q, c_prev, preferred_element_type=F32)
        qn = jnp.sum(p, axis=-1, keepdims=True) + wint * jnp.dot(q, nvec.astype(BF16), preferred_element_type=F32)
        den = jnp.maximum(jnp.abs(qn), flo)
        ht = num / jnp.concatenate([den, den], axis=1)
        y = ht * lax.rsqrt(jnp.mean(ht * ht, axis=-1, keepdims=True) + EPS) * hn_ref[...]
        o_ref[pl.ds(r0, L), :] = (og_ref[pl.ds(r0, L), :].astype(F32) * y).astype(BF16)

        kwT = kT.astype(F32) * kw
        cmat = jnp.concatenate([dec, dec], axis=1) * cmat + jnp.dot(kwT.astype(BF16), v, preferred_element_type=F32)
        nvec = dec * nvec + jnp.sum(kwT, axis=-1, keepdims=True)
        return cmat, nvec

    lax.fori_loop(0, nc, chunk, (jnp.zeros((dk, dv), F32), jnp.zeros((dk, LANES), F32)), unroll=2)


def _mlstm_core(q, kT, v, og, h_norm, negm, winter, floor, u, kw, dec):
    b, s, _ = q.shape
    dk, dv = MLSTM_QK_DIM, MLSTM_V_DIM
    d = MLSTM_HEADS * dv
    col = pl.BlockSpec((None, None, s, LANES), lambda i, j: (i, j, 0, 0))
    rowspec = pl.BlockSpec((None, None, 1, s), lambda i, j: (i, j, 0, 0))
    per_head_row = lambda a: a.reshape(b, MLSTM_HEADS, 1, s)
    return pl.pallas_call(
        _mlstm_core_body,
        out_shape=jax.ShapeDtypeStruct((b, s, d), BF16),
        grid=(b, MLSTM_HEADS),
        in_specs=[
            pl.BlockSpec((None, s, dk), lambda i, j: (i, 0, j)),
            pl.BlockSpec((None, dk, s), lambda i, j: (i, j, 0)),
            pl.BlockSpec((None, s, dv), lambda i, j: (i, 0, j)),
            pl.BlockSpec((None, s, dv), lambda i, j: (i, 0, j)),
            pl.BlockSpec((1, dv), lambda i, j: (0, j)),
            col, col, col, rowspec, rowspec, rowspec,
        ],
        out_specs=pl.BlockSpec((None, s, dv), lambda i, j: (i, 0, j)),
        compiler_params=_params("parallel", "parallel"),
        name="mlstm_core",
    )(q, kT, v, og, h_norm.reshape(1, d), negm, winter, floor, per_head_row(u), per_head_row(kw), per_head_row(dec))


def _mlstm_layer(x3d, mix_norm, w_in, b_gates, h_norm, w_out):
    b, s, d = x3d.shape
    nh = MLSTM_HEADS
    nqk = nh * MLSTM_QK_DIM
    wq = w_in[:, :nqk].astype(BF16)
    wkT = w_in[:, nqk:2 * nqk].T.astype(BF16)
    wv = w_in[:, 2 * nqk:2 * nqk + d].astype(BF16)
    wo = w_in[:, 2 * nqk + d:2 * nqk + 2 * d].astype(BF16)
    wg = w_in[:, 2 * nqk + 2 * d:]
    pad = lambda w: jnp.pad(w, ((0, 0), (0, LANES - nh))).astype(BF16)
    padb = lambda v: jnp.pad(v.astype(F32), (0, LANES - nh)).reshape(1, LANES)
    bo = b_gates[:d].astype(F32).reshape(1, d)
    q, kT, v, og, gi, gf = _mlstm_proj(
        x3d, mix_norm, wq, wkT, wv, wo, pad(wg[:, :nh]), pad(wg[:, nh:]),
        bo, padb(b_gates[d:d + nh]), padb(b_gates[d + nh:]))
    negm, winter, floor, u, kw, dec = _mlstm_gates(gi, gf)
    hgated = _mlstm_core(q, kT, v, og, h_norm.astype(F32), negm, winter, floor, u, kw, dec)
    out = _proj_res(hgated.reshape(b * s, d), w_out.astype(BF16), jnp.zeros((d,), F32), x3d.reshape(b * s, d))
    return out.reshape(b, s, d)


def kernel(x, l0_mix_norm, l0_attn_w_qkv, l0_attn_q_norm, l0_attn_k_norm, l0_attn_sinks, l0_attn_w_o, l0_mlp_norm, l0_mlp_w1, l0_mlp_w2, l1_mix_norm, l1_conv_w_in, l1_conv_b_in, l1_conv_dw, l1_conv_dw_b, l1_conv_ln_g, l1_conv_ln_b, l1_conv_w_out, l1_conv_b_out, l1_mlp_norm, l1_mlp_w1, l1_mlp_w2, l2_mix_norm, l2_mlstm_w_in, l2_mlstm_b_gates, l2_mlstm_h_norm, l2_mlstm_w_out, l2_mlp_norm, l2_mlp_w1, l2_mlp_w2, l3_mix_norm, l3_attn_w_qkv, l3_attn_q_norm, l3_attn_k_norm, l3_attn_sinks, l3_attn_w_o, l3_mlp_norm, l3_mlp_w1, l3_mlp_w2):
    b, s, d = x.shape

    def mlp(x3d, norm, w1, w2):
        return _mlp(x3d.reshape(b * s, d), norm.astype(F32), w1.astype(BF16), w2.astype(BF16)).reshape(b, s, d)

    x = _attention_layer(x, l0_mix_norm, l0_attn_w_qkv, l0_attn_q_norm, l0_attn_k_norm, l0_attn_sinks, l0_attn_w_o)
    x = mlp(x, l0_mlp_norm, l0_mlp_w1, l0_mlp_w2)
    x = _conv_layer(x, l1_mix_norm, l1_conv_w_in, l1_conv_b_in, l1_conv_dw, l1_conv_dw_b, l1_conv_ln_g,
                    l1_conv_ln_b, l1_conv_w_out, l1_conv_b_out)
    x = mlp(x, l1_mlp_norm, l1_mlp_w1, l1_mlp_w2)
    x = _mlstm_layer(x, l2_mix_norm, l2_mlstm_w_in, l2_mlstm_b_gates, l2_mlstm_h_norm, l2_mlstm_w_out)
    x = mlp(x, l2_mlp_norm, l2_mlp_w1, l2_mlp_w2)
    x = _attention_layer(x, l3_mix_norm, l3_attn_w_qkv, l3_attn_q_norm, l3_attn_k_norm, l3_attn_sinks, l3_attn_w_o)
    x = mlp(x, l3_mlp_norm, l3_mlp_w1, l3_mlp_w2)
    return x
```

```python
import functools

import jax
import jax.numpy as jnp
from jax import lax
from jax.experimental import pallas as pl
from jax.experimental.pallas import tpu as pltpu

F32 = jnp.float32
BF16 = jnp.bfloat16

EPS = 1e-6
NEG = -1e30
LOG2E = 1.4426950408889634

ATTN_HEADS = 32
ATTN_KV_HEADS = 4
ATTN_GROUP = ATTN_HEADS // ATTN_KV_HEADS
ATTN_HEAD_DIM = 64
ATTN_BLOCK = 128
CHUNK = 64
CONV_WIDTH = 31
MLSTM_HEADS = 8
MLSTM_QK_DIM = 128
MLSTM_V_DIM = 256
MLSTM_CHUNK = 128

LANES = 128
VMEM_LIMIT_BYTES = 60000 * 1024

CONV_HALO = 32


def _params(*sem):
    return pltpu.CompilerParams(dimension_semantics=sem, vmem_limit_bytes=VMEM_LIMIT_BYTES)


def _resident(shape):
    nd = len(shape)
    return pl.BlockSpec(shape, lambda *_: (0,) * nd, pipeline_mode=pl.Buffered(1))


def _rms_norm_rows(x, gain):
    return x * lax.rsqrt(jnp.mean(x * x, axis=-1, keepdims=True) + EPS) * gain


def _mlp_body(x_ref, g_ref, w1_ref, w2_ref, o_ref, h_ref):
    @pl.when(pl.program_id(1) == 0)
    def _():
        x = x_ref[...]
        h_ref[...] = _rms_norm_rows(x, g_ref[...]).astype(BF16)
        o_ref[...] = x

    a = jnp.dot(h_ref[...], w1_ref[...], preferred_element_type=F32)
    a = jnp.square(jnp.maximum(a, 0.0)).astype(BF16)
    o_ref[...] += jnp.dot(a, w2_ref[...], preferred_element_type=F32)


def _mlp(x2d, gain, w1, w2, *, bm=512, fc=1024):
    m, d = x2d.shape
    ff = w1.shape[1]
    return pl.pallas_call(
        _mlp_body,
        out_shape=jax.ShapeDtypeStruct((m, d), F32),
        grid=(m // bm, ff // fc),
        in_specs=[
            pl.BlockSpec((bm, d), lambda i, j: (i, 0)),
            pl.BlockSpec((1, d), lambda i, j: (0, 0)),
            pl.BlockSpec((d, fc), lambda i, j: (0, j)),
            pl.BlockSpec((fc, d), lambda i, j: (j, 0)),
        ],
        out_specs=pl.BlockSpec((bm, d), lambda i, j: (i, 0)),
        scratch_shapes=[pltpu.VMEM((bm, d), BF16)],
        compiler_params=_params("parallel", "arbitrary"),
        name="mlp",
    )(x2d, gain.reshape(1, d), w1, w2)


def _proj_res_body(a_ref, w_ref, b_ref, x_ref, o_ref):
    acc = jnp.dot(a_ref[...], w_ref[...], preferred_element_type=F32)
    o_ref[...] = x_ref[...] + acc + b_ref[...]


def _proj_res(a2d, w, bias, x2d, *, bm=512):
    m, k = a2d.shape
    d = w.shape[1]
    return pl.pallas_call(
        _proj_res_body,
        out_shape=jax.ShapeDtypeStruct((m, d), F32),
        grid=(m // bm,),
        in_specs=[
            pl.BlockSpec((bm, k), lambda i: (i, 0)),
            _resident((k, d)),
            _resident((1, d)),
            pl.BlockSpec((bm, d), lambda i: (i, 0)),
        ],
        out_specs=pl.BlockSpec((bm, d), lambda i: (i, 0)),
        compiler_params=_params("parallel"),
        name="proj_res",
    )(a2d, w, bias.reshape(1, d), x2d)


def _attn_qkv_body(x_ref, g_ref, wqvT_ref, wk_ref, kgain_ref, qT_ref, k_ref, vT_ref):
    h = _rms_norm_rows(x_ref[...], g_ref[...]).astype(BF16)
    nq = ATTN_HEADS * ATTN_HEAD_DIM
    qvT = lax.dot_general(wqvT_ref[...], h, (((1,), (1,)), ((), ())),
                          preferred_element_type=F32)
    bm = h.shape[0]
    q3 = qvT[:nq].reshape(ATTN_HEADS, ATTN_HEAD_DIM, bm)
    inv = lax.rsqrt(jnp.mean(q3 * q3, axis=1, keepdims=True) + EPS) * (ATTN_HEAD_DIM ** -0.5)
    qT_ref[...] = (q3 * inv).reshape(nq, bm).astype(BF16)
    vT_ref[...] = qvT[nq:].astype(BF16)
    kf = jnp.dot(h, wk_ref[...], preferred_element_type=F32)
    for j in range(ATTN_KV_HEADS):
        kj = kf[:, j * LANES:j * LANES + ATTN_HEAD_DIM]
        kn = kj * lax.rsqrt(jnp.mean(kj * kj, axis=-1, keepdims=True) + EPS)
        k_ref[j] = (kn * kgain_ref[...]).astype(BF16)


def _attn_qkv(x3d, gain, wqvT, wk_pad, kgain, *, bm=512):
    b, s, d = x3d.shape
    nq = ATTN_HEADS * ATTN_HEAD_DIM
    nkv = ATTN_KV_HEADS * ATTN_HEAD_DIM
    return pl.pallas_call(
        _attn_qkv_body,
        out_shape=(
            jax.ShapeDtypeStruct((b, nq, s), BF16),
            jax.ShapeDtypeStruct((b, ATTN_KV_HEADS, s, ATTN_HEAD_DIM), BF16),
            jax.ShapeDtypeStruct((b, nkv, s), BF16),
        ),
        grid=(b, s // bm),
        in_specs=[
            pl.BlockSpec((None, bm, d), lambda i, j: (i, j, 0)),
            _resident((1, d)),
            _resident((nq + nkv, d)),
            _resident((d, ATTN_KV_HEADS * LANES)),
            _resident((1, ATTN_HEAD_DIM)),
        ],
        out_specs=(
            pl.BlockSpec((None, nq, bm), lambda i, j: (i, 0, j)),
            pl.BlockSpec((None, ATTN_KV_HEADS, bm, ATTN_HEAD_DIM), lambda i, j: (i, 0, j, 0)),
            pl.BlockSpec((None, nkv, bm), lambda i, j: (i, 0, j)),
        ),
        compiler_params=_params("parallel", "parallel"),
        name="attn_qkv",
    )(x3d, gain.reshape(1, d), wqvT, wk_pad, kgain.reshape(1, ATTN_HEAD_DIM))


def _attn_core_body(qT_ref, kp_ref, ko_ref, vTp_ref, vTo_ref, sink_ref, bias_ref, oT_ref):
    qb = ATTN_BLOCK
    hd = ATTN_HEAD_DIM
    for h in range(ATTN_KV_HEADS):
        kb = jnp.concatenate([kp_ref[h], ko_ref[h]], axis=0)
        q = jnp.concatenate(
            [qT_ref[(h * ATTN_GROUP + g) * hd:(h * ATTN_GROUP + g + 1) * hd, :] for g in range(ATTN_GROUP)],
            axis=1)
        s = jnp.dot(kb, q, preferred_element_type=F32) + bias_ref[...]
        sk = sink_ref[h]
        m = jnp.maximum(jnp.max(s, axis=0, keepdims=True), sk)
        p = jnp.exp2(s - m)
        l = jnp.sum(p, axis=0, keepdims=True) + jnp.exp2(sk - m)
        vb = jnp.concatenate([vTp_ref[h * hd:(h + 1) * hd, :], vTo_ref[h * hd:(h + 1) * hd, :]], axis=1)
        o = jnp.dot(vb, p.astype(BF16), preferred_element_type=F32) / l
        for g in range(ATTN_GROUP):
            r0 = (h * ATTN_GROUP + g) * hd
            oT_ref[r0:r0 + hd, :] = o[:, g * qb:(g + 1) * qb].astype(BF16)


def _attn_band_bias():
    qb = ATTN_BLOCK
    shape = (2, 2 * qb, ATTN_GROUP * qb)
    first = lax.broadcasted_iota(jnp.int32, shape, 0) == 0
    row = lax.broadcasted_iota(jnp.int32, shape, 1)
    col = lax.broadcasted_iota(jnp.int32, shape, 2)
    kc = row // CHUNK
    qc = (col % qb) // CHUNK
    allowed = (kc >= qc) & (kc <= qc + 2) & ((row >= qb) | ~first)
    return jnp.where(allowed, 0.0, NEG).astype(F32)


def _attn_core(qT, k, vT, sink_rows):
    b, nq, s = qT.shape
    nkv = vT.shape[1]
    qb = ATTN_BLOCK
    prev = lambda j: jnp.maximum(j - 1, 0)
    bias = _attn_band_bias()
    return pl.pallas_call(
        _attn_core_body,
        out_shape=jax.ShapeDtypeStruct((b, nq, s), BF16),
        grid=(b, s // qb),
        in_specs=[
            pl.BlockSpec((None, nq, qb), lambda i, j: (i, 0, j)),
            pl.BlockSpec((None, ATTN_KV_HEADS, qb, ATTN_HEAD_DIM), lambda i, j: (i, 0, prev(j), 0)),
            pl.BlockSpec((None, ATTN_KV_HEADS, qb, ATTN_HEAD_DIM), lambda i, j: (i, 0, j, 0)),
            pl.BlockSpec((None, nkv, qb), lambda i, j: (i, 0, prev(j))),
            pl.BlockSpec((None, nkv, qb), lambda i, j: (i, 0, j)),
            _resident((ATTN_KV_HEADS, 1, ATTN_GROUP * qb)),
            pl.BlockSpec((None, 2 * qb, ATTN_GROUP * qb), lambda i, j: (jnp.minimum(j, 1), 0, 0)),
        ],
        out_specs=pl.BlockSpec((None, nq, qb), lambda i, j: (i, 0, j)),
        compiler_params=_params("parallel", "parallel"),
        name="attn_core",
    )(qT, k, k, vT, vT, sink_rows, bias)


def _proj_res_t_body(aT_ref, w_ref, x_ref, o_ref):
    acc = lax.dot_general(aT_ref[...], w_ref[...], (((0,), (0,)), ((), ())), preferred_element_type=F32)
    o_ref[...] = x_ref[...] + acc


def _proj_res_t(aT, w, x3d, *, bm=512):
    b, k, s = aT.shape
    d = w.shape[1]
    return pl.pallas_call(
        _proj_res_t_body,
        out_shape=jax.ShapeDtypeStruct((b, s, d), F32),
        grid=(b, s // bm),
        in_specs=[
            pl.BlockSpec((None, k, bm), lambda i, j: (i, 0, j)),
            _resident((k, d)),
            pl.BlockSpec((None, bm, d), lambda i, j: (i, j, 0)),
        ],
        out_specs=pl.BlockSpec((None, bm, d), lambda i, j: (i, j, 0)),
        compiler_params=_params("parallel", "parallel"),
        name="proj_res_t",
    )(aT, w, x3d)


def _attention_layer(x3d, mix_norm, w_qkv, q_norm, k_norm, sinks, w_o):
    b, s, d = x3d.shape
    nq = ATTN_HEADS * ATTN_HEAD_DIM
    nkv = ATTN_KV_HEADS * ATTN_HEAD_DIM
    wq, wk, wv = w_qkv[:, :nq], w_qkv[:, nq:nq + nkv], w_qkv[:, nq + nkv:]
    wqvT = jnp.concatenate([wq, wv], axis=1).T.astype(BF16)
    wk_pad = jnp.pad(wk.reshape(d, ATTN_KV_HEADS, ATTN_HEAD_DIM),
                     ((0, 0), (0, 0), (0, LANES - ATTN_HEAD_DIM))).reshape(d, ATTN_KV_HEADS * LANES).astype(BF16)
    kgain = (q_norm * k_norm).astype(F32) * LOG2E
    sink_rows = jnp.repeat(sinks.astype(F32).reshape(ATTN_KV_HEADS, 1, ATTN_GROUP) * LOG2E, ATTN_BLOCK, axis=2)
    qT, k, vT = _attn_qkv(x3d, mix_norm, wqvT, wk_pad, kgain)
    oT = _attn_core(qT, k, vT, sink_rows)
    return _proj_res_t(oT, w_o.astype(BF16), x3d)


def _conv_body(x_ref, g_ref, win_ref, bin_ref, dw_ref, dwb_ref, lng_ref, lnb_ref, wout_ref, bout_ref,
               o_ref, gbuf, cbuf, ybuf):
    bm, d = x_ref.shape
    nslab = d // LANES

    @pl.when(pl.program_id(1) == 0)
    def _():
        gbuf[:, 0:CONV_HALO, :] = jnp.zeros((nslab, CONV_HALO, LANES), F32)

    x = x_ref[...]
    h = _rms_norm_rows(x, g_ref[...]).astype(BF16)
    u = jnp.dot(h, win_ref[...], preferred_element_type=F32) + bin_ref[...]
    glu = u[:, :d] * jax.nn.sigmoid(u[:, d:])
    for j in range(nslab):
        gbuf[j, CONV_HALO:CONV_HALO + bm, :] = glu[:, j * LANES:(j + 1) * LANES]

    def slab(j, carry):
        acc = jnp.broadcast_to(dwb_ref[j], (bm, LANES))
        taps = dw_ref[j]
        for t in range(CONV_WIDTH):
            r0 = CONV_HALO - (CONV_WIDTH - 1) + t
            acc = acc + taps[t:t + 1, :] * gbuf[j, pl.ds(r0, bm), :]
        cbuf[j] = acc
        gbuf[j, 0:CONV_HALO, :] = gbuf[j, pl.ds(bm, CONV_HALO), :]
        return carry

    lax.fori_loop(0, nslab, slab, 0)

    tot = cbuf[0]
    for j in range(1, nslab):
        tot = tot + cbuf[j]
    mu = jnp.sum(tot, axis=-1, keepdims=True) * (1.0 / d)
    sq = None
    for j in range(nslab):
        c = cbuf[j] - mu
        sq = c * c if sq is None else sq + c * c
    inv = lax.rsqrt(jnp.sum(sq, axis=-1, keepdims=True) * (1.0 / d) + EPS)
    for j in range(nslab):
        y = (cbuf[j] - mu) * inv * lng_ref[:, j * LANES:(j + 1) * LANES] + lnb_ref[:, j * LANES:(j + 1) * LANES]
        ybuf[:, j * LANES:(j + 1) * LANES] = (y * jax.nn.sigmoid(y)).astype(BF16)
    o_ref[...] = x + jnp.dot(ybuf[...], wout_ref[...], preferred_element_type=F32) + bout_ref[...]


def _conv_layer(x3d, mix_norm, w_in, b_in, dw, dw_b, ln_g, ln_b, w_out, b_out, *, bm=256):
    b, s, d = x3d.shape
    nslab = d // LANES
    dw_slabs = dw.astype(F32).reshape(CONV_WIDTH, nslab, LANES).transpose(1, 0, 2)
    dwb_slabs = dw_b.astype(F32).reshape(nslab, 1, LANES)
    row = lambda v: v.astype(F32).reshape(1, -1)
    return pl.pallas_call(
        _conv_body,
        out_shape=jax.ShapeDtypeStruct((b, s, d), F32),
        grid=(b, s // bm),
        in_specs=[
            pl.BlockSpec((None, bm, d), lambda i, j: (i, j, 0)),
            _resident((1, d)),
            _resident((d, 2 * d)),
            _resident((1, 2 * d)),
            _resident((nslab, CONV_WIDTH, LANES)),
            _resident((nslab, 1, LANES)),
            _resident((1, d)),
            _resident((1, d)),
            _resident((d, d)),
            _resident((1, d)),
        ],
        out_specs=pl.BlockSpec((None, bm, d), lambda i, j: (i, j, 0)),
        scratch_shapes=[
            pltpu.VMEM((nslab, CONV_HALO + bm, LANES), F32),
            pltpu.VMEM((nslab, bm, LANES), F32),
            pltpu.VMEM((bm, d), BF16),
        ],
        compiler_params=_params("parallel", "arbitrary"),
        name="conv_module",
    )(x3d, row(mix_norm), w_in.astype(BF16), row(b_in), dw_slabs, dwb_slabs, row(ln_g), row(ln_b),
      w_out.astype(BF16), row(b_out))


def _mlstm_proj_body(x_ref, g_ref, wq_ref, wkT_ref, wv_ref, wo_ref, wi_ref, wf_ref, bo_ref, bi_ref, bf_ref,
                     q_ref, kT_ref, v_ref, og_ref, gi_ref, gf_ref):
    h = _rms_norm_rows(x_ref[...], g_ref[...]).astype(BF16)
    q = jnp.dot(h, wq_ref[...], preferred_element_type=F32)
    q_ref[...] = (q * (MLSTM_QK_DIM ** -0.5)).astype(BF16)
    kT_ref[...] = lax.dot_general(wkT_ref[...], h, (((1,), (1,)), ((), ())),
                                  preferred_element_type=F32).astype(BF16)
    v_ref[...] = jnp.dot(h, wv_ref[...], preferred_element_type=F32).astype(BF16)
    o_pre = jnp.dot(h, wo_ref[...], preferred_element_type=F32) + bo_ref[...]
    og_ref[...] = jax.nn.sigmoid(o_pre).astype(BF16)
    gi_ref[...] = jnp.dot(h, wi_ref[...], preferred_element_type=F32) + bi_ref[...]
    gf_ref[...] = jnp.dot(h, wf_ref[...], preferred_element_type=F32) + bf_ref[...]


def _mlstm_proj(x3d, gain, wq, wkT, wv, wo, wi, wf, bo, bi, bf, *, bm=256):
    b, s, d = x3d.shape
    nqk = MLSTM_HEADS * MLSTM_QK_DIM
    tok = lambda n: pl.BlockSpec((None, bm, n), lambda i, j: (i, j, 0))
    return pl.pallas_call(
        _mlstm_proj_body,
        out_shape=(
            jax.ShapeDtypeStruct((b, s, nqk), BF16),
            jax.ShapeDtypeStruct((b, nqk, s), BF16),
            jax.ShapeDtypeStruct((b, s, d), BF16),
            jax.ShapeDtypeStruct((b, s, d), BF16),
            jax.ShapeDtypeStruct((b, s, LANES), F32),
            jax.ShapeDtypeStruct((b, s, LANES), F32),
        ),
        grid=(b, s // bm),
        in_specs=[
            tok(d),
            _resident((1, d)),
            _resident((d, nqk)),
            _resident((nqk, d)),
            _resident((d, d)),
            _resident((d, d)),
            _resident((d, LANES)),
            _resident((d, LANES)),
            _resident((1, d)),
            _resident((1, LANES)),
            _resident((1, LANES)),
        ],
        out_specs=(
            tok(nqk),
            pl.BlockSpec((None, nqk, bm), lambda i, j: (i, 0, j)),
            tok(d),
            tok(d),
            tok(LANES),
            tok(LANES),
        ),
        compiler_params=_params("parallel", "parallel"),
        name="mlstm_proj",
    )(x3d, gain.reshape(1, d), wq, wkT, wv, wo, wi, wf, bo, bi, bf)


def _scan_rows(x, op, fill, period):
    r = lax.broadcasted_iota(jnp.int32, x.shape, 0) % period
    k = 1
    while k < period:
        x = op(x, jnp.where(r >= k, pltpu.roll(x, k, axis=0), fill))
        k *= 2
    return x


def _mlstm_gates_body(gi_ref, gf_ref, negm_ref, winter_ref, floor_ref, u_ref, kw_ref, dec_ref):
    s = gi_ref.shape[0]
    L = MLSTM_CHUNK
    nc = s // L
    gf = gf_ref[...]
    logf = jnp.minimum(gf, 0.0) - jnp.log(1.0 + jnp.exp(-jnp.abs(gf)))
    b = _scan_rows(logf, jnp.add, 0.0, L)
    u = gi_ref[...] - b
    cm = _scan_rows(u, jnp.maximum, -jnp.inf, L)
    m = jnp.zeros((1, LANES), F32)
    for c in range(nc):
        rows = slice(c * L, (c + 1) * L)
        cm_c = cm[rows]
        b_c = b[rows]
        u_c = u[rows]
        big_m = jnp.maximum(m, cm_c)
        m_last = big_m[L - 1:L, :]
        negm = -big_m
        wint = jnp.exp(m - big_m)
        flo = jnp.exp(-(b_c + big_m))
        for hh in range(MLSTM_HEADS):
            negm_ref[hh, rows, :] = jnp.broadcast_to(negm[:, hh:hh + 1], (L, LANES))
            winter_ref[hh, rows, :] = jnp.broadcast_to(wint[:, hh:hh + 1], (L, LANES))
            floor_ref[hh, rows, :] = jnp.broadcast_to(flo[:, hh:hh + 1], (L, LANES))
        uT = u_c.T
        mlT = jnp.broadcast_to(m_last, (L, LANES)).T
        mpT = jnp.broadcast_to(m, (L, LANES)).T
        u_ref[:, rows] = uT[:MLSTM_HEADS]
        kw_ref[:, rows] = jnp.exp(uT - mlT)[:MLSTM_HEADS]
        dec_ref[:, rows] = jnp.exp(mpT - mlT)[:MLSTM_HEADS]
        m = b_c[L - 1:L, :] + m_last


def _mlstm_gates(gi, gf):
    b, s, _ = gi.shape
    hcol = jax.ShapeDtypeStruct((b, MLSTM_HEADS, s, LANES), F32)
    hrow = jax.ShapeDtypeStruct((b, MLSTM_HEADS, s), F32)
    tok = pl.BlockSpec((None, s, LANES), lambda i: (i, 0, 0))
    col = pl.BlockSpec((None, MLSTM_HEADS, s, LANES), lambda i: (i, 0, 0, 0))
    rowspec = pl.BlockSpec((None, MLSTM_HEADS, s), lambda i: (i, 0, 0))
    return pl.pallas_call(
        _mlstm_gates_body,
        out_shape=(hcol, hcol, hcol, hrow, hrow, hrow),
        grid=(b,),
        in_specs=[tok, tok],
        out_specs=(col, col, col, rowspec, rowspec, rowspec),
        compiler_params=_params("parallel"),
        name="mlstm_gates",
    )(gi, gf)


def _mlstm_core_body(q_ref, kT_ref, v_ref, og_ref, hn_ref, negm_ref, winter_ref, floor_ref,
                     u_ref, kw_ref, dec_ref, o_ref):
    L = MLSTM_CHUNK
    dk, dv = MLSTM_QK_DIM, MLSTM_V_DIM
    nc = q_ref.shape[0] // L
    tril = lax.broadcasted_iota(jnp.int32, (L, L), 0) >= lax.broadcasted_iota(jnp.int32, (L, L), 1)

    def chunk(c, carry):
        cmat, nvec = carry
        r0 = pl.multiple_of(c * L, L)
        q = q_ref[pl.ds(r0, L), :]
        kT = kT_ref[:, pl.ds(r0, L)]
        v = v_ref[pl.ds(r0, L), :]
        u = u_ref[:, pl.ds(r0, L)]
        kw = kw_ref[:, pl.ds(r0, L)]
        dec = dec_ref[:, pl.ds(r0, L)]
        negm = negm_ref[pl.ds(r0, L), :]
        wint = winter_ref[pl.ds(r0, L), :]
        flo = floor_ref[pl.ds(r0, L), :]

        dmat = jnp.where(tril, jnp.exp(u + negm), 0.0)
        p = jnp.dot(q, kT, preferred_element_type=F32) * dmat
        c_prev = cmat.astype(BF16)
        num = jnp.dot(p.astype(BF16), v, preferred_element_type=F32)
        num = num + jnp.concatenate([wint, wint], axis=1) * jnp.dot(q, c_prev, preferred_element_type=F32)
        qn = jnp.sum(p, axis=-1, keepdims=True) + wint * jnp.dot(q, nvec.astype(BF16), preferred_element_type=F32)
        den = jnp.maximum(jnp.abs(qn), flo)
        r = 1.0 / den
        scale = r * lax.rsqrt(jnp.mean(num * num, axis=-1, keepdims=True) * r * r + EPS)
        y = num * jnp.concatenate([scale, scale], axis=1) * hn_ref[...]
        o_ref[pl.ds(r0, L), :] = (og_ref[pl.ds(r0, L), :].astype(F32) * y).astype(BF16)

        kwT = kT.astype(F32) * kw
        cmat = jnp.concatenate([dec, dec], axis=1) * cmat + jnp.dot(kwT.astype(BF16), v, preferred_element_type=F32)
        nvec = dec * nvec + jnp.sum(kwT, axis=-1, keepdims=True)
        return cmat, nvec

    lax.fori_loop(0, nc, chunk, (jnp.zeros((dk, dv), F32), jnp.zeros((dk, LANES), F32)), unroll=True)


def _mlstm_core(q, kT, v, og, h_norm, negm, winter, floor, u, kw, dec):
    b, s, _ = q.shape
    dk, dv = MLSTM_QK_DIM, MLSTM_V_DIM
    d = MLSTM_HEADS * dv
    col = pl.BlockSpec((None, None, s, LANES), lambda i, j: (i, j, 0, 0))
    rowspec = pl.BlockSpec((None, None, 1, s), lambda i, j: (i, j, 0, 0))
    per_head_row = lambda a: a.reshape(b, MLSTM_HEADS, 1, s)
    return pl.pallas_call(
        _mlstm_core_body,
        out_shape=jax.ShapeDtypeStruct((b, s, d), BF16),
        grid=(b, MLSTM_HEADS),
        in_specs=[
            pl.BlockSpec((None, s, dk), lambda i, j: (i, 0, j)),
            pl.BlockSpec((None, dk, s), lambda i, j: (i, j, 0)),
            pl.BlockSpec((None, s, dv), lambda i, j: (i, 0, j)),
            pl.BlockSpec((None, s, dv), lambda i, j: (i, 0, j)),
            pl.BlockSpec((1, dv), lambda i, j: (0, j)),
            col, col, col, rowspec, rowspec, rowspec,
        ],
        out_specs=pl.BlockSpec((None, s, dv), lambda i, j: (i, 0, j)),
        compiler_params=_params("parallel", "parallel"),
        name="mlstm_core",
    )(q, kT, v, og, h_norm.reshape(1, d), negm, winter, floor, per_head_row(u), per_head_row(kw), per_head_row(dec))


def _mlstm_layer(x3d, mix_norm, w_in, b_gates, h_norm, w_out):
    b, s, d = x3d.shape
    nh = MLSTM_HEADS
    nqk = nh * MLSTM_QK_DIM
    wq = w_in[:, :nqk].astype(BF16)
    wkT = w_in[:, nqk:2 * nqk].T.astype(BF16)
    wv = w_in[:, 2 * nqk:2 * nqk + d].astype(BF16)
    wo = w_in[:, 2 * nqk + d:2 * nqk + 2 * d].astype(BF16)
    wg = w_in[:, 2 * nqk + 2 * d:]
    pad = lambda w: jnp.pad(w, ((0, 0), (0, LANES - nh))).astype(BF16)
    padb = lambda v: jnp.pad(v.astype(F32), (0, LANES - nh)).reshape(1, LANES)
    bo = b_gates[:d].astype(F32).reshape(1, d)
    q, kT, v, og, gi, gf = _mlstm_proj(
        x3d, mix_norm, wq, wkT, wv, wo, pad(wg[:, :nh]), pad(wg[:, nh:]),
        bo, padb(b_gates[d:d + nh]), padb(b_gates[d + nh:]))
    negm, winter, floor, u, kw, dec = _mlstm_gates(gi, gf)
    hgated = _mlstm_core(q, kT, v, og, h_norm.astype(F32), negm, winter, floor, u, kw, dec)
    out = _proj_res(hgated.reshape(b * s, d), w_out.astype(BF16), jnp.zeros((d,), F32), x3d.reshape(b * s, d))
    return out.reshape(b, s, d)


def kernel(x, l0_mix_norm, l0_attn_w_qkv, l0_attn_q_norm, l0_attn_k_norm, l0_attn_sinks, l0_attn_w_o, l0_mlp_norm, l0_mlp_w1, l0_mlp_w2, l1_mix_norm, l1_conv_w_in, l1_conv_b_in, l1_conv_dw, l1_conv_dw_b, l1_conv_ln_g, l1_conv_ln_b, l1_conv_w_out, l1_conv_b_out, l1_mlp_norm, l1_mlp_w1, l1_mlp_w2, l2_mix_norm, l2_mlstm_w_in, l2_mlstm_b_gates, l2_mlstm_h_norm, l2_mlstm_w_out, l2_mlp_norm, l2_mlp_w1, l2_mlp_w2, l3_mix_norm, l3_attn_w_qkv, l3_attn_q_norm, l3_attn_k_norm, l3_attn_sinks, l3_attn_w_o, l3_mlp_norm, l3_mlp_w1, l3_mlp_w2):
    b, s, d = x.shape

    def mlp(x3d, norm, w1, w2):
        return _mlp(x3d.reshape(b * s, d), norm.astype(F32), w1.astype(BF16), w2.astype(BF16)).reshape(b, s, d)

    x = _attention_layer(x, l0_mix_norm, l0_attn_w_qkv, l0_attn_q_norm, l0_attn_k_norm, l0_attn_sinks, l0_attn_w_o)
    x = mlp(x, l0_mlp_norm, l0_mlp_w1, l0_mlp_w2)
    x = _conv_layer(x, l1_mix_norm, l1_conv_w_in, l1_conv_b_in, l1_conv_dw, l1_conv_dw_b, l1_conv_ln_g,
                    l1_conv_ln_b, l1_conv_w_out, l1_conv_b_out)
    x = mlp(x, l1_mlp_norm, l1_mlp_w1, l1_mlp_w2)
    x = _mlstm_layer(x, l2_mix_norm, l2_mlstm_w_in, l2_mlstm_b_gates, l2_mlstm_h_norm, l2_mlstm_w_out)
    x = mlp(x, l2_mlp_norm, l2_mlp_w1, l2_mlp_w2)
    x = _attention_layer(x, l3_mix_norm, l3_attn_w_qkv, l3_attn_q_norm, l3_attn_k_norm, l3_attn_sinks, l3_attn_w_o)
    x = mlp(x, l3_mlp_norm, l3_mlp_w1, l3_mlp_w2)
    return x
```

```python
import functools

import jax
import jax.numpy as jnp
from jax import lax
from jax.experimental import pallas as pl
from jax.experimental.pallas import tpu as pltpu

F32 = jnp.float32
BF16 = jnp.bfloat16

EPS = 1e-6
NEG = -1e30
LOG2E = 1.4426950408889634

ATTN_HEADS = 32
ATTN_KV_HEADS = 4
ATTN_GROUP = ATTN_HEADS // ATTN_KV_HEADS
ATTN_HEAD_DIM = 64
ATTN_BLOCK = 128
CHUNK = 64
CONV_WIDTH = 31
MLSTM_HEADS = 8
MLSTM_QK_DIM = 128
MLSTM_V_DIM = 256
MLSTM_CHUNK = 128

LANES = 128
VMEM_LIMIT_BYTES = 60000 * 1024

QKV_ROW_SPLIT = 2
ATTN_SCORE_LOOKAHEAD = 2
MLSTM_LOOKAHEAD = 1
CONV_HALO = 32


def _params(*sem):
    return pltpu.CompilerParams(dimension_semantics=sem, vmem_limit_bytes=VMEM_LIMIT_BYTES)


def _resident(shape):
    nd = len(shape)
    return pl.BlockSpec(shape, lambda *_: (0,) * nd, pipeline_mode=pl.Buffered(1))


def _rms_norm_rows(x, gain):
    return x * lax.rsqrt(jnp.mean(x * x, axis=-1, keepdims=True) + EPS) * gain


BF16_SUBLANES = 16


def _cast_specs(w, grid, flat_index):
    rows, cols = w.shape
    steps = 1
    for g in grid:
        steps *= g
    per_step = rows // steps
    assert per_step * steps == rows and per_step >= 1
    blk = max(per_step, BF16_SUBLANES)
    share = blk // per_step
    index_map = lambda *g: (flat_index(*g) // share, 0)
    spec = pl.BlockSpec((blk, cols), index_map)
    return spec, spec, jax.ShapeDtypeStruct((rows, cols), BF16)


def _mlp_body(x_ref, g_ref, w1_ref, w2_ref, *rest, n_cast):
    cast_in, o_ref, cast_out, h_ref = rest[:n_cast], rest[n_cast], rest[n_cast + 1:2 * n_cast + 1], rest[-1]
    for src, dst in zip(cast_in, cast_out):
        dst[...] = src[...].astype(BF16)

    def up_down(h):
        a = jnp.dot(h, w1_ref[...], preferred_element_type=F32)
        a = jnp.square(jnp.maximum(a, 0.0)).astype(BF16)
        return jnp.dot(a, w2_ref[...], preferred_element_type=F32)

    @pl.when(pl.program_id(1) == 0)
    def _():
        x = x_ref[...]
        h = _rms_norm_rows(x, g_ref[...]).astype(BF16)
        h_ref[...] = h
        o_ref[...] = x + up_down(h)

    @pl.when(pl.program_id(1) != 0)
    def _():
        o_ref[...] += up_down(h_ref[...])


def _mlp(x2d, gain, w1, w2, cast=(), *, bm=512, fc=1024):
    m, d = x2d.shape
    ff = w1.shape[1]
    grid = (m // bm, ff // fc)
    side = [_cast_specs(w, grid, lambda i, j: i * grid[1] + j) for w in cast]
    return pl.pallas_call(
        functools.partial(_mlp_body, n_cast=len(cast)),
        out_shape=[jax.ShapeDtypeStruct((m, d), F32)] + [s[2] for s in side],
        grid=grid,
        in_specs=[
            pl.BlockSpec((bm, d), lambda i, j: (i, 0)),
            pl.BlockSpec((1, d), lambda i, j: (0, 0)),
            pl.BlockSpec((d, fc), lambda i, j: (0, j)),
            pl.BlockSpec((fc, d), lambda i, j: (j, 0)),
        ] + [s[0] for s in side],
        out_specs=[pl.BlockSpec((bm, d), lambda i, j: (i, 0))] + [s[1] for s in side],
        scratch_shapes=[pltpu.VMEM((bm, d), BF16)],
        compiler_params=_params("arbitrary", "arbitrary"),
        name="mlp",
    )(x2d, gain.reshape(1, d), w1, w2, *cast)


def _proj_res_body(a_ref, w_ref, b_ref, x_ref, o_ref):
    acc = jnp.dot(a_ref[...], w_ref[...], preferred_element_type=F32)
    o_ref[...] = x_ref[...] + acc + b_ref[...]


def _proj_res(a2d, w, bias, x2d, *, bm=512):
    m, k = a2d.shape
    d = w.shape[1]
    return pl.pallas_call(
        _proj_res_body,
        out_shape=jax.ShapeDtypeStruct((m, d), F32),
        grid=(m // bm,),
        in_specs=[
            pl.BlockSpec((bm, k), lambda i: (i, 0)),
            _resident((k, d)),
            _resident((1, d)),
            pl.BlockSpec((bm, d), lambda i: (i, 0)),
        ],
        out_specs=pl.BlockSpec((bm, d), lambda i: (i, 0)),
        compiler_params=_params("parallel"),
        name="proj_res",
    )(a2d, w, bias.reshape(1, d), x2d)


def _attn_qkv_body(x_ref, g_ref, wqvT_ref, wk_ref, kgain_ref, *rest, n_cast):
    cast_in, (qT_ref, k_ref, vT_ref), cast_out = rest[:n_cast], rest[n_cast:n_cast + 3], rest[n_cast + 3:]
    for src, dst in zip(cast_in, cast_out):
        dst[...] = src[...].astype(BF16)
    nq = ATTN_HEADS * ATTN_HEAD_DIM
    bm = x_ref.shape[0]
    rt = bm // QKV_ROW_SPLIT
    prods = []
    for r in range(QKV_ROW_SPLIT):
        h = _rms_norm_rows(x_ref[r * rt:(r + 1) * rt, :], g_ref[...]).astype(BF16)
        qvT = lax.dot_general(wqvT_ref[...], h, (((1,), (1,)), ((), ())),
                              preferred_element_type=F32)
        kf = jnp.dot(h, wk_ref[...], preferred_element_type=F32)
        prods.append((qvT, kf))
    for r, (qvT, kf) in enumerate(prods):
        q3 = qvT[:nq].reshape(ATTN_HEADS, ATTN_HEAD_DIM, rt)
        inv = lax.rsqrt(jnp.mean(q3 * q3, axis=1, keepdims=True) + EPS) * (ATTN_HEAD_DIM ** -0.5)
        qT_ref[:, r * rt:(r + 1) * rt] = (q3 * inv).reshape(nq, rt).astype(BF16)
        vT_ref[:, r * rt:(r + 1) * rt] = qvT[nq:].astype(BF16)
        for j in range(ATTN_KV_HEADS):
            kj = kf[:, j * LANES:j * LANES + ATTN_HEAD_DIM]
            kn = kj * lax.rsqrt(jnp.mean(kj * kj, axis=-1, keepdims=True) + EPS)
            k_ref[j, r * rt:(r + 1) * rt, :] = (kn * kgain_ref[...]).astype(BF16)


def _attn_qkv(x3d, gain, wqvT, wk_pad, kgain, cast=(), *, bm=512):
    b, s, d = x3d.shape
    nq = ATTN_HEADS * ATTN_HEAD_DIM
    nkv = ATTN_KV_HEADS * ATTN_HEAD_DIM
    grid = (b, s // bm)
    side = [_cast_specs(w, grid, lambda i, j: i * grid[1] + j) for w in cast]
    return pl.pallas_call(
        functools.partial(_attn_qkv_body, n_cast=len(cast)),
        out_shape=[
            jax.ShapeDtypeStruct((b, nq, s), BF16),
            jax.ShapeDtypeStruct((b, ATTN_KV_HEADS, s, ATTN_HEAD_DIM), BF16),
            jax.ShapeDtypeStruct((b, nkv, s), BF16),
        ] + [sd[2] for sd in side],
        grid=grid,
        in_specs=[
            pl.BlockSpec((None, bm, d), lambda i, j: (i, j, 0)),
            _resident((1, d)),
            _resident((nq + nkv, d)),
            _resident((d, ATTN_KV_HEADS * LANES)),
            _resident((1, ATTN_HEAD_DIM)),
        ] + [sd[0] for sd in side],
        out_specs=[
            pl.BlockSpec((None, nq, bm), lambda i, j: (i, 0, j)),
            pl.BlockSpec((None, ATTN_KV_HEADS, bm, ATTN_HEAD_DIM), lambda i, j: (i, 0, j, 0)),
            pl.BlockSpec((None, nkv, bm), lambda i, j: (i, 0, j)),
        ] + [sd[1] for sd in side],
        compiler_params=_params("arbitrary", "arbitrary"),
        name="attn_qkv",
    )(x3d, gain.reshape(1, d), wqvT, wk_pad, kgain.reshape(1, ATTN_HEAD_DIM), *cast)


def _attn_core_body(qT_ref, kp_ref, ko_ref, vTp_ref, vTo_ref, sink_ref, oT_ref):
    n = pl.program_id(1)
    qb, hd, ck, grp = ATTN_BLOCK, ATTN_HEAD_DIM, CHUNK, ATTN_GROUP
    nvis = 3 * ck
    low = lax.broadcasted_iota(jnp.int32, (hd, LANES), 1) < ck
    no_prev = jnp.where(n == 0, NEG, 0.0)
    zeros = jnp.zeros((ck, grp * ck), BF16)
    nchunk = qb // ck

    def scores(h, c):
        tiles = [qT_ref[(h * grp + g) * hd:(h * grp + g + 1) * hd, :] for g in range(grp)]
        if c == 0:
            pairs = [jnp.where(low, tiles[2 * j], pltpu.roll(tiles[2 * j + 1], ck, axis=1)) for j in range(grp // 2)]
        else:
            pairs = [jnp.where(low, pltpu.roll(tiles[2 * j], ck, axis=1), tiles[2 * j + 1]) for j in range(grp // 2)]
        q = jnp.concatenate(pairs, axis=1)
        kb = jnp.concatenate([kp_ref[h], ko_ref[h]], axis=0)
        return jnp.dot(kb[c * ck:c * ck + nvis, :], q, preferred_element_type=F32)

    def attend(h, c, s):
        nprev = qb - c * ck
        s = jnp.concatenate([s[:nprev] + no_prev, s[nprev:]], axis=0)
        sk = sink_ref[h]
        m = jnp.maximum(jnp.max(s, axis=0, keepdims=True), sk)
        p = jnp.exp2(s - m)
        l = jnp.sum(p, axis=0, keepdims=True) + jnp.exp2(sk - m)
        pb = p.astype(BF16)
        pb = jnp.concatenate([pb, zeros] if c == 0 else [zeros, pb], axis=0)
        vb = jnp.concatenate([vTp_ref[h * hd:(h + 1) * hd, :], vTo_ref[h * hd:(h + 1) * hd, :]], axis=1)
        return (jnp.dot(vb, pb, preferred_element_type=F32) / l).astype(BF16)

    subs = [(h, c) for h in range(ATTN_KV_HEADS) for c in range(nchunk)]
    ahead = ATTN_SCORE_LOOKAHEAD
    pending = [scores(*sub) for sub in subs[:ahead]]
    outs = {}
    for i, (h, c) in enumerate(subs):
        s_cur = pending.pop(0)
        if i + ahead < len(subs):
            pending.append(scores(*subs[i + ahead]))
        outs[c] = attend(h, c, s_cur)
        if c == nchunk - 1:
            for j in range(grp // 2):
                a0 = outs[0][:, j * LANES:(j + 1) * LANES]
                a1 = outs[1][:, j * LANES:(j + 1) * LANES]
                r0 = (h * grp + 2 * j) * hd
                oT_ref[r0:r0 + hd, :] = jnp.where(low, a0, pltpu.roll(a1, ck, axis=1))
                oT_ref[r0 + hd:r0 + 2 * hd, :] = jnp.where(low, pltpu.roll(a0, ck, axis=1), a1)


def _attn_core(qT, k, vT, sink_rows):
    b, nq, s = qT.shape
    nkv = vT.shape[1]
    qb = ATTN_BLOCK
    prev = lambda j: jnp.maximum(j - 1, 0)
    return pl.pallas_call(
        _attn_core_body,
        out_shape=jax.ShapeDtypeStruct((b, nq, s), BF16),
        grid=(b, s // qb),
        in_specs=[
            pl.BlockSpec((None, nq, qb), lambda i, j: (i, 0, j)),
            pl.BlockSpec((None, ATTN_KV_HEADS, qb, ATTN_HEAD_DIM), lambda i, j: (i, 0, prev(j), 0)),
            pl.BlockSpec((None, ATTN_KV_HEADS, qb, ATTN_HEAD_DIM), lambda i, j: (i, 0, j, 0)),
            pl.BlockSpec((None, nkv, qb), lambda i, j: (i, 0, prev(j))),
            pl.BlockSpec((None, nkv, qb), lambda i, j: (i, 0, j)),
            _resident((ATTN_KV_HEADS, 1, ATTN_GROUP * CHUNK)),
        ],
        out_specs=pl.BlockSpec((None, nq, qb), lambda i, j: (i, 0, j)),
        compiler_params=_params("parallel", "parallel"),
        name="attn_core",
    )(qT, k, k, vT, vT, sink_rows)


def _proj_res_t_body(aT_ref, w_ref, x_ref, o_ref):
    acc = lax.dot_general(aT_ref[...], w_ref[...], (((0,), (0,)), ((), ())), preferred_element_type=F32)
    o_ref[...] = x_ref[...] + acc


def _proj_res_t(aT, w, x3d, *, bm=512):
    b, k, s = aT.shape
    d = w.shape[1]
    return pl.pallas_call(
        _proj_res_t_body,
        out_shape=jax.ShapeDtypeStruct((b, s, d), F32),
        grid=(b, s // bm),
        in_specs=[
            pl.BlockSpec((None, k, bm), lambda i, j: (i, 0, j)),
            _resident((k, d)),
            pl.BlockSpec((None, bm, d), lambda i, j: (i, j, 0)),
        ],
        out_specs=pl.BlockSpec((None, bm, d), lambda i, j: (i, j, 0)),
        compiler_params=_params("parallel", "parallel"),
        name="proj_res_t",
    )(aT, w, x3d)


def _attention_layer(x3d, mix_norm, w_qkv, q_norm, k_norm, sinks, w_o, cast=()):
    b, s, d = x3d.shape
    nq = ATTN_HEADS * ATTN_HEAD_DIM
    nkv = ATTN_KV_HEADS * ATTN_HEAD_DIM
    wq, wk, wv = w_qkv[:, :nq], w_qkv[:, nq:nq + nkv], w_qkv[:, nq + nkv:]
    wqvT = jnp.concatenate([wq, wv], axis=1).T.astype(BF16)
    wk_pad = jnp.pad(wk.reshape(d, ATTN_KV_HEADS, ATTN_HEAD_DIM),
                     ((0, 0), (0, 0), (0, LANES - ATTN_HEAD_DIM))).reshape(d, ATTN_KV_HEADS * LANES).astype(BF16)
    kgain = (q_norm * k_norm).astype(F32) * LOG2E
    sink_rows = jnp.repeat(sinks.astype(F32).reshape(ATTN_KV_HEADS, 1, ATTN_GROUP) * LOG2E, CHUNK, axis=2)
    qT, k, vT, *casted = _attn_qkv(x3d, mix_norm, wqvT, wk_pad, kgain, cast)
    oT = _attn_core(qT, k, vT, sink_rows)
    return _proj_res_t(oT, w_o.astype(BF16), x3d), casted


def _conv_body(x_ref, g_ref, win_ref, bin_ref, dw_ref, dwb_ref, lng_ref, lnb_ref, wout_ref, bout_ref,
               o_ref, gbuf, cbuf, ybuf):
    bm, d = x_ref.shape
    nslab = d // LANES

    @pl.when(pl.program_id(1) == 0)
    def _():
        gbuf[:, 0:CONV_HALO, :] = jnp.zeros((nslab, CONV_HALO, LANES), F32)

    x = x_ref[...]
    h = _rms_norm_rows(x, g_ref[...]).astype(BF16)
    u = jnp.dot(h, win_ref[...], preferred_element_type=F32) + bin_ref[...]
    glu = u[:, :d] * jax.nn.sigmoid(u[:, d:])
    for j in range(nslab):
        gbuf[j, CONV_HALO:CONV_HALO + bm, :] = glu[:, j * LANES:(j + 1) * LANES]

    def slab(j, carry):
        acc = jnp.broadcast_to(dwb_ref[j], (bm, LANES))
        taps = dw_ref[j]
        for t in range(CONV_WIDTH):
            r0 = CONV_HALO - (CONV_WIDTH - 1) + t
            acc = acc + taps[t:t + 1, :] * gbuf[j, pl.ds(r0, bm), :]
        cbuf[j] = acc
        gbuf[j, 0:CONV_HALO, :] = gbuf[j, pl.ds(bm, CONV_HALO), :]
        return carry

    lax.fori_loop(0, nslab, slab, 0)

    tot = cbuf[0]
    for j in range(1, nslab):
        tot = tot + cbuf[j]
    mu = jnp.sum(tot, axis=-1, keepdims=True) * (1.0 / d)
    sq = None
    for j in range(nslab):
        c = cbuf[j] - mu
        sq = c * c if sq is None else sq + c * c
    inv = lax.rsqrt(jnp.sum(sq, axis=-1, keepdims=True) * (1.0 / d) + EPS)
    for j in range(nslab):
        y = (cbuf[j] - mu) * inv * lng_ref[:, j * LANES:(j + 1) * LANES] + lnb_ref[:, j * LANES:(j + 1) * LANES]
        ybuf[:, j * LANES:(j + 1) * LANES] = (y * jax.nn.sigmoid(y)).astype(BF16)
    o_ref[...] = x + jnp.dot(ybuf[...], wout_ref[...], preferred_element_type=F32) + bout_ref[...]


def _conv_layer(x3d, mix_norm, w_in, b_in, dw, dw_b, ln_g, ln_b, w_out, b_out, *, bm=256):
    b, s, d = x3d.shape
    nslab = d // LANES
    dw_slabs = dw.astype(F32).reshape(CONV_WIDTH, nslab, LANES).transpose(1, 0, 2)
    dwb_slabs = dw_b.astype(F32).reshape(nslab, 1, LANES)
    row = lambda v: v.astype(F32).reshape(1, -1)
    return pl.pallas_call(
        _conv_body,
        out_shape=jax.ShapeDtypeStruct((b, s, d), F32),
        grid=(b, s // bm),
        in_specs=[
            pl.BlockSpec((None, bm, d), lambda i, j: (i, j, 0)),
            _resident((1, d)),
            _resident((d, 2 * d)),
            _resident((1, 2 * d)),
            _resident((nslab, CONV_WIDTH, LANES)),
            _resident((nslab, 1, LANES)),
            _resident((1, d)),
            _resident((1, d)),
            _resident((d, d)),
            _resident((1, d)),
        ],
        out_specs=pl.BlockSpec((None, bm, d), lambda i, j: (i, j, 0)),
        scratch_shapes=[
            pltpu.VMEM((nslab, CONV_HALO + bm, LANES), F32),
            pltpu.VMEM((nslab, bm, LANES), F32),
            pltpu.VMEM((bm, d), BF16),
        ],
        compiler_params=_params("parallel", "arbitrary"),
        name="conv_module",
    )(x3d, row(mix_norm), w_in.astype(BF16), row(b_in), dw_slabs, dwb_slabs, row(ln_g), row(ln_b),
      w_out.astype(BF16), row(b_out))


def _mlstm_proj_body(x_ref, g_ref, wq_ref, wkT_ref, wv_ref, wo_ref, wi_ref, wf_ref, bo_ref, bi_ref, bf_ref,
                     q_ref, kT_ref, v_ref, og_ref, gi_ref, gf_ref):
    h = _rms_norm_rows(x_ref[...], g_ref[...]).astype(BF16)
    q = jnp.dot(h, wq_ref[...], preferred_element_type=F32)
    q_ref[...] = (q * (MLSTM_QK_DIM ** -0.5)).astype(BF16)
    kT_ref[...] = lax.dot_general(wkT_ref[...], h, (((1,), (1,)), ((), ())),
                                  preferred_element_type=F32).astype(BF16)
    v_ref[...] = jnp.dot(h, wv_ref[...], preferred_element_type=F32).astype(BF16)
    o_pre = jnp.dot(h, wo_ref[...], preferred_element_type=F32) + bo_ref[...]
    og_ref[...] = jax.nn.sigmoid(o_pre).astype(BF16)
    gi_ref[...] = jnp.dot(h, wi_ref[...], preferred_element_type=F32) + bi_ref[...]
    gf_ref[...] = jnp.dot(h, wf_ref[...], preferred_element_type=F32) + bf_ref[...]


def _mlstm_proj(x3d, gain, wq, wkT, wv, wo, wi, wf, bo, bi, bf, *, bm=256):
    b, s, d = x3d.shape
    nqk = MLSTM_HEADS * MLSTM_QK_DIM
    tok = lambda n: pl.BlockSpec((None, bm, n), lambda i, j: (i, j, 0))
    return pl.pallas_call(
        _mlstm_proj_body,
        out_shape=(
            jax.ShapeDtypeStruct((b, s, nqk), BF16),
            jax.ShapeDtypeStruct((b, nqk, s), BF16),
            jax.ShapeDtypeStruct((b, s, d), BF16),
            jax.ShapeDtypeStruct((b, s, d), BF16),
            jax.ShapeDtypeStruct((b, s, LANES), F32),
            jax.ShapeDtypeStruct((b, s, LANES), F32),
        ),
        grid=(b, s // bm),
        in_specs=[
            tok(d),
            _resident((1, d)),
            _resident((d, nqk)),
            _resident((nqk, d)),
            _resident((d, d)),
            _resident((d, d)),
            _resident((d, LANES)),
            _resident((d, LANES)),
            _resident((1, d)),
            _resident((1, LANES)),
            _resident((1, LANES)),
        ],
        out_specs=(
            tok(nqk),
            pl.BlockSpec((None, nqk, bm), lambda i, j: (i, 0, j)),
            tok(d),
            tok(d),
            tok(LANES),
            tok(LANES),
        ),
        compiler_params=_params("parallel", "parallel"),
        name="mlstm_proj",
    )(x3d, gain.reshape(1, d), wq, wkT, wv, wo, wi, wf, bo, bi, bf)


def _scan_rows(x, op, fill, period):
    r = lax.broadcasted_iota(jnp.int32, x.shape, 0) % period
    k = 1
    while k < period:
        x = op(x, jnp.where(r >= k, pltpu.roll(x, k, axis=0), fill))
        k *= 2
    return x


def _mlstm_gates_body(gi_ref, gf_ref, negm_ref, winter_ref, floor_ref, u_ref, kw_ref, dec_ref):
    s = gi_ref.shape[0]
    L = MLSTM_CHUNK
    nc = s // L
    gf = gf_ref[...]
    logf = jnp.minimum(gf, 0.0) - jnp.log(1.0 + jnp.exp(-jnp.abs(gf)))
    b = _scan_rows(logf, jnp.add, 0.0, L)
    u = gi_ref[...] - b
    cm = _scan_rows(u, jnp.maximum, -jnp.inf, L)
    m = jnp.zeros((1, LANES), F32)
    for c in range(nc):
        rows = slice(c * L, (c + 1) * L)
        cm_c = cm[rows]
        b_c = b[rows]
        u_c = u[rows]
        big_m = jnp.maximum(m, cm_c)
        m_last = big_m[L - 1:L, :]
        negm = -big_m
        wint = jnp.exp(m - big_m)
        flo = jnp.exp(-(b_c + big_m))
        for hh in range(MLSTM_HEADS):
            negm_ref[hh, rows, :] = jnp.broadcast_to(negm[:, hh:hh + 1], (L, LANES))
            winter_ref[hh, rows, :] = jnp.broadcast_to(wint[:, hh:hh + 1], (L, LANES))
            floor_ref[hh, rows, :] = jnp.broadcast_to(flo[:, hh:hh + 1], (L, LANES))
        uT = u_c.T
        mlT = jnp.broadcast_to(m_last, (L, LANES)).T
        mpT = jnp.broadcast_to(m, (L, LANES)).T
        u_ref[:, rows] = uT[:MLSTM_HEADS]
        kw_ref[:, rows] = jnp.exp(uT - mlT)[:MLSTM_HEADS]
        dec_ref[:, rows] = jnp.exp(mpT - mlT)[:MLSTM_HEADS]
        m = b_c[L - 1:L, :] + m_last


def _mlstm_gates(gi, gf):
    b, s, _ = gi.shape
    hcol = jax.ShapeDtypeStruct((b, MLSTM_HEADS, s, LANES), F32)
    hrow = jax.ShapeDtypeStruct((b, MLSTM_HEADS, s), F32)
    tok = pl.BlockSpec((None, s, LANES), lambda i: (i, 0, 0))
    col = pl.BlockSpec((None, MLSTM_HEADS, s, LANES), lambda i: (i, 0, 0, 0))
    rowspec = pl.BlockSpec((None, MLSTM_HEADS, s), lambda i: (i, 0, 0))
    return pl.pallas_call(
        _mlstm_gates_body,
        out_shape=(hcol, hcol, hcol, hrow, hrow, hrow),
        grid=(b,),
        in_specs=[tok, tok],
        out_specs=(col, col, col, rowspec, rowspec, rowspec),
        compiler_params=_params("parallel"),
        name="mlstm_gates",
    )(gi, gf)


def _mlstm_core_body(q_ref, kT_ref, v_ref, og_ref, hn_ref, negm_ref, winter_ref, floor_ref,
                     u_ref, kw_ref, dec_ref, o_ref):
    L = MLSTM_CHUNK
    dk, dv = MLSTM_QK_DIM, MLSTM_V_DIM
    nc = q_ref.shape[0] // L
    tril = lax.broadcasted_iota(jnp.int32, (L, L), 0) >= lax.broadcasted_iota(jnp.int32, (L, L), 1)

    def state_free(c):
        rows = slice(c * L, (c + 1) * L)
        kT = kT_ref[:, rows]
        qk = jnp.dot(q_ref[rows, :], kT, preferred_element_type=F32)
        kwT = kT.astype(F32) * kw_ref[:, rows]
        inc = jnp.dot(kwT.astype(BF16), v_ref[rows, :], preferred_element_type=F32)
        return qk, inc, jnp.sum(kwT, axis=-1, keepdims=True)

    cmat = jnp.zeros((dk, dv), F32)
    nvec = jnp.zeros((dk, LANES), F32)
    pending = [state_free(c) for c in range(min(MLSTM_LOOKAHEAD, nc))]
    for c in range(nc):
        rows = slice(c * L, (c + 1) * L)
        qk, inc, ksum = pending.pop(0)
        if c + MLSTM_LOOKAHEAD < nc:
            pending.append(state_free(c + MLSTM_LOOKAHEAD))
        q = q_ref[rows, :]
        wint = winter_ref[rows, :]
        inter = jnp.dot(q, cmat.astype(BF16), preferred_element_type=F32)
        qn_inter = jnp.dot(q, nvec.astype(BF16), preferred_element_type=F32)
        dmat = jnp.where(tril, jnp.exp(u_ref[:, rows] + negm_ref[rows, :]), 0.0)
        p = qk * dmat
        num = jnp.dot(p.astype(BF16), v_ref[rows, :], preferred_element_type=F32)
        num = num + jnp.concatenate([wint, wint], axis=1) * inter
        qn = jnp.sum(p, axis=-1, keepdims=True) + wint * qn_inter
        den = jnp.maximum(jnp.abs(qn), floor_ref[rows, :])
        r = 1.0 / den
        scale = r * lax.rsqrt(jnp.mean(num * num, axis=-1, keepdims=True) * r * r + EPS)
        y = num * jnp.concatenate([scale, scale], axis=1) * hn_ref[...]
        o_ref[rows, :] = (og_ref[rows, :].astype(F32) * y).astype(BF16)

        dec = dec_ref[:, rows]
        cmat = jnp.concatenate([dec, dec], axis=1) * cmat + inc
        nvec = dec * nvec + ksum


def _mlstm_core(q, kT, v, og, h_norm, negm, winter, floor, u, kw, dec):
    b, s, _ = q.shape
    dk, dv = MLSTM_QK_DIM, MLSTM_V_DIM
    d = MLSTM_HEADS * dv
    col = pl.BlockSpec((None, None, s, LANES), lambda i, j: (i, j, 0, 0))
    rowspec = pl.BlockSpec((None, None, 1, s), lambda i, j: (i, j, 0, 0))
    per_head_row = lambda a: a.reshape(b, MLSTM_HEADS, 1, s)
    return pl.pallas_call(
        _mlstm_core_body,
        out_shape=jax.ShapeDtypeStruct((b, s, d), BF16),
        grid=(b, MLSTM_HEADS),
        in_specs=[
            pl.BlockSpec((None, s, dk), lambda i, j: (i, 0, j)),
            pl.BlockSpec((None, dk, s), lambda i, j: (i, j, 0)),
            pl.BlockSpec((None, s, dv), lambda i, j: (i, 0, j)),
            pl.BlockSpec((None, s, dv), lambda i, j: (i, 0, j)),
            pl.BlockSpec((1, dv), lambda i, j: (0, j)),
            col, col, col, rowspec, rowspec, rowspec,
        ],
        out_specs=pl.BlockSpec((None, s, dv), lambda i, j: (i, 0, j)),
        compiler_params=_params("parallel", "parallel"),
        name="mlstm_core",
    )(q, kT, v, og, h_norm.reshape(1, d), negm, winter, floor, per_head_row(u), per_head_row(kw), per_head_row(dec))


def _mlstm_layer(x3d, mix_norm, w_in, b_gates, h_norm, w_out):
    b, s, d = x3d.shape
    nh = MLSTM_HEADS
    nqk = nh * MLSTM_QK_DIM
    wq = w_in[:, :nqk].astype(BF16)
    wkT = w_in[:, nqk:2 * nqk].T.astype(BF16)
    wv = w_in[:, 2 * nqk:2 * nqk + d].astype(BF16)
    wo = w_in[:, 2 * nqk + d:2 * nqk + 2 * d].astype(BF16)
    wg = w_in[:, 2 * nqk + 2 * d:]
    pad = lambda w: jnp.pad(w, ((0, 0), (0, LANES - nh))).astype(BF16)
    padb = lambda v: jnp.pad(v.astype(F32), (0, LANES - nh)).reshape(1, LANES)
    bo = b_gates[:d].astype(F32).reshape(1, d)
    q, kT, v, og, gi, gf = _mlstm_proj(
        x3d, mix_norm, wq, wkT, wv, wo, pad(wg[:, :nh]), pad(wg[:, nh:]),
        bo, padb(b_gates[d:d + nh]), padb(b_gates[d + nh:]))
    negm, winter, floor, u, kw, dec = _mlstm_gates(gi, gf)
    hgated = _mlstm_core(q, kT, v, og, h_norm.astype(F32), negm, winter, floor, u, kw, dec)
    out = _proj_res(hgated.reshape(b * s, d), w_out.astype(BF16), jnp.zeros((d,), F32), x3d.reshape(b * s, d))
    return out.reshape(b, s, d)


def kernel(x, l0_mix_norm, l0_attn_w_qkv, l0_attn_q_norm, l0_attn_k_norm, l0_attn_sinks, l0_attn_w_o, l0_mlp_norm, l0_mlp_w1, l0_mlp_w2, l1_mix_norm, l1_conv_w_in, l1_conv_b_in, l1_conv_dw, l1_conv_dw_b, l1_conv_ln_g, l1_conv_ln_b, l1_conv_w_out, l1_conv_b_out, l1_mlp_norm, l1_mlp_w1, l1_mlp_w2, l2_mix_norm, l2_mlstm_w_in, l2_mlstm_b_gates, l2_mlstm_h_norm, l2_mlstm_w_out, l2_mlp_norm, l2_mlp_w1, l2_mlp_w2, l3_mix_norm, l3_attn_w_qkv, l3_attn_q_norm, l3_attn_k_norm, l3_attn_sinks, l3_attn_w_o, l3_mlp_norm, l3_mlp_w1, l3_mlp_w2):
    b, s, d = x.shape

    def mlp(x3d, norm, w1, w2, cast=()):
        out, *casted = _mlp(x3d.reshape(b * s, d), norm.astype(F32), w1, w2, cast)
        return out.reshape(b, s, d), casted

    x, (w1, w2) = _attention_layer(x, l0_mix_norm, l0_attn_w_qkv, l0_attn_q_norm, l0_attn_k_norm, l0_attn_sinks,
                                   l0_attn_w_o, cast=(l0_mlp_w1, l0_mlp_w2))
    x, (conv_w_in, conv_w_out, w1, w2) = mlp(x, l0_mlp_norm, w1, w2,
                                             cast=(l1_conv_w_in, l1_conv_w_out, l1_mlp_w1, l1_mlp_w2))
    x = _conv_layer(x, l1_mix_norm, conv_w_in, l1_conv_b_in, l1_conv_dw, l1_conv_dw_b, l1_conv_ln_g,
                    l1_conv_ln_b, conv_w_out, l1_conv_b_out)
    x, (mlstm_w_out, w1, w2) = mlp(x, l1_mlp_norm, w1, w2, cast=(l2_mlstm_w_out, l2_mlp_w1, l2_mlp_w2))
    x = _mlstm_layer(x, l2_mix_norm, l2_mlstm_w_in, l2_mlstm_b_gates, l2_mlstm_h_norm, mlstm_w_out)
    x, (attn_w_o, w1, w2) = mlp(x, l2_mlp_norm, w1, w2, cast=(l3_attn_w_o, l3_mlp_w1, l3_mlp_w2))
    x, _ = _attention_layer(x, l3_mix_norm, l3_attn_w_qkv, l3_attn_q_norm, l3_attn_k_norm, l3_attn_sinks, attn_w_o)
    x, _ = mlp(x, l3_mlp_norm, w1, w2)
    return x
```

```python
import jax
import jax.numpy as jnp
from jax import lax
from jax.experimental import pallas as pl
from jax.experimental.pallas import tpu as pltpu

F32 = jnp.float32
BF16 = jnp.bfloat16

EPS = 1e-6
NEG = -1e30
LOG2E = 1.4426950408889634

ATTN_HEADS = 32
ATTN_KV_HEADS = 4
ATTN_GROUP = ATTN_HEADS // ATTN_KV_HEADS
ATTN_HEAD_DIM = 64
ATTN_BLOCK = 128
CHUNK = 64
CONV_WIDTH = 31
MLSTM_HEADS = 8
MLSTM_QK_DIM = 128
MLSTM_V_DIM = 256
MLSTM_CHUNK = 128

LANES = 128
VMEM_LIMIT_BYTES = 60000 * 1024

QKV_ROW_SPLIT = 2
ATTN_SCORE_LOOKAHEAD = 2
MLSTM_LOOKAHEAD = 1
CONV_HALO = 32


def _params(*sem):
    return pltpu.CompilerParams(dimension_semantics=sem, vmem_limit_bytes=VMEM_LIMIT_BYTES)


def _resident(shape):
    nd = len(shape)
    return pl.BlockSpec(shape, lambda *_: (0,) * nd, pipeline_mode=pl.Buffered(1))


def _rms_norm_rows(x, gain):
    return x * lax.rsqrt(jnp.mean(x * x, axis=-1, keepdims=True) + EPS) * gain


BF16_SUBLANES = 16


def _cast_specs(w, grid):
    rows, cols = w.shape
    steps = 1
    for g in grid:
        steps *= g
    per_step = rows // steps
    assert per_step * steps == rows and per_step >= 1
    blk = max(per_step, BF16_SUBLANES)
    share = blk // per_step

    def index_map(*g):
        flat = g[0]
        for n, i in zip(grid[1:], g[1:]):
            flat = flat * n + i
        return flat // share, 0

    return pl.BlockSpec((blk, cols), index_map)


def _call_with_casts(body, args, cast, *, grid, in_specs, out_specs, out_shape, scratch_shapes=(), name):
    n_in, n_out, n_cast = len(in_specs), len(out_specs), len(cast)

    def wrapped(*refs):
        cast_in = refs[n_in:n_in + n_cast]
        cast_out = refs[n_in + n_cast + n_out:n_in + 2 * n_cast + n_out]
        for src, dst in zip(cast_in, cast_out):
            dst[...] = src[...].astype(BF16)
        body(*refs[:n_in], *refs[n_in + n_cast:n_in + n_cast + n_out], *refs[n_in + 2 * n_cast + n_out:])

    side = [_cast_specs(w, grid) for w in cast]
    res = pl.pallas_call(
        wrapped,
        out_shape=list(out_shape) + [jax.ShapeDtypeStruct(w.shape, BF16) for w in cast],
        grid=grid,
        in_specs=list(in_specs) + side,
        out_specs=list(out_specs) + side,
        scratch_shapes=list(scratch_shapes),
        compiler_params=_params(*(("arbitrary",) * len(grid))),
        name=name,
    )(*args, *cast)
    return res[:n_out], res[n_out:]


def _mlp_body(x_ref, g_ref, w1_ref, w2_ref, o_ref, h_ref):
    def up_down(h):
        a = jnp.dot(h, w1_ref[...], preferred_element_type=F32)
        a = jnp.square(jnp.maximum(a, 0.0)).astype(BF16)
        return jnp.dot(a, w2_ref[...], preferred_element_type=F32)

    @pl.when(pl.program_id(1) == 0)
    def _():
        x = x_ref[...]
        h = _rms_norm_rows(x, g_ref[...]).astype(BF16)
        h_ref[...] = h
        o_ref[...] = x + up_down(h)

    @pl.when(pl.program_id(1) != 0)
    def _():
        o_ref[...] += up_down(h_ref[...])


def _mlp(x2d, gain, w1, w2, *, bm=1024, fc=1024):
    m, d = x2d.shape
    ff = w1.shape[1]
    return pl.pallas_call(
        _mlp_body,
        out_shape=jax.ShapeDtypeStruct((m, d), F32),
        grid=(m // bm, ff // fc),
        in_specs=[
            pl.BlockSpec((bm, d), lambda i, j: (i, 0)),
            pl.BlockSpec((1, d), lambda i, j: (0, 0)),
            pl.BlockSpec((d, fc), lambda i, j: (0, j)),
            pl.BlockSpec((fc, d), lambda i, j: (j, 0)),
        ],
        out_specs=pl.BlockSpec((bm, d), lambda i, j: (i, 0)),
        scratch_shapes=[pltpu.VMEM((bm, d), BF16)],
        compiler_params=_params("parallel", "arbitrary"),
        name="mlp",
    )(x2d, gain.reshape(1, d), w1, w2)


def _proj_res_body(a_ref, w_ref, b_ref, x_ref, o_ref):
    acc = jnp.dot(a_ref[...], w_ref[...], preferred_element_type=F32)
    o_ref[...] = x_ref[...] + acc + b_ref[...]


def _proj_res(a2d, w, bias, x2d, *, bm=512):
    m, k = a2d.shape
    d = w.shape[1]
    return pl.pallas_call(
        _proj_res_body,
        out_shape=jax.ShapeDtypeStruct((m, d), F32),
        grid=(m // bm,),
        in_specs=[
            pl.BlockSpec((bm, k), lambda i: (i, 0)),
            _resident((k, d)),
            _resident((1, d)),
            pl.BlockSpec((bm, d), lambda i: (i, 0)),
        ],
        out_specs=pl.BlockSpec((bm, d), lambda i: (i, 0)),
        compiler_params=_params("parallel"),
        name="proj_res",
    )(a2d, w, bias.reshape(1, d), x2d)


def _attn_qkv_body(x_ref, g_ref, wqvT_ref, wk_ref, kgain_ref, qT_ref, k_ref, vT_ref):
    nq = ATTN_HEADS * ATTN_HEAD_DIM
    bm = x_ref.shape[0]
    rt = bm // QKV_ROW_SPLIT
    prods = []
    for r in range(QKV_ROW_SPLIT):
        h = _rms_norm_rows(x_ref[r * rt:(r + 1) * rt, :], g_ref[...]).astype(BF16)
        qvT = lax.dot_general(wqvT_ref[...], h, (((1,), (1,)), ((), ())),
                              preferred_element_type=F32)
        kf = jnp.dot(h, wk_ref[...], preferred_element_type=F32)
        prods.append((qvT, kf))
    for r, (qvT, kf) in enumerate(prods):
        q3 = qvT[:nq].reshape(ATTN_HEADS, ATTN_HEAD_DIM, rt)
        inv = lax.rsqrt(jnp.mean(q3 * q3, axis=1, keepdims=True) + EPS) * (ATTN_HEAD_DIM ** -0.5)
        qT_ref[:, r * rt:(r + 1) * rt] = (q3 * inv).reshape(nq, rt).astype(BF16)
        vT_ref[:, r * rt:(r + 1) * rt] = qvT[nq:].astype(BF16)
        for j in range(ATTN_KV_HEADS):
            kj = kf[:, j * LANES:j * LANES + ATTN_HEAD_DIM]
            kn = kj * lax.rsqrt(jnp.mean(kj * kj, axis=-1, keepdims=True) + EPS)
            k_ref[j, r * rt:(r + 1) * rt, :] = (kn * kgain_ref[...]).astype(BF16)


def _attn_qkv(x3d, gain, wqvT, wk_pad, kgain, cast=(), *, bm=512):
    b, s, d = x3d.shape
    nq = ATTN_HEADS * ATTN_HEAD_DIM
    nkv = ATTN_KV_HEADS * ATTN_HEAD_DIM
    return _call_with_casts(
        _attn_qkv_body,
        (x3d, gain.reshape(1, d), wqvT, wk_pad, kgain.reshape(1, ATTN_HEAD_DIM)),
        cast,
        out_shape=[
            jax.ShapeDtypeStruct((b, nq, s), BF16),
            jax.ShapeDtypeStruct((b, ATTN_KV_HEADS, s, ATTN_HEAD_DIM), BF16),
            jax.ShapeDtypeStruct((b, nkv, s), BF16),
        ],
        grid=(b, s // bm),
        in_specs=[
            pl.BlockSpec((None, bm, d), lambda i, j: (i, j, 0)),
            _resident((1, d)),
            _resident((nq + nkv, d)),
            _resident((d, ATTN_KV_HEADS * LANES)),
            _resident((1, ATTN_HEAD_DIM)),
        ],
        out_specs=[
            pl.BlockSpec((None, nq, bm), lambda i, j: (i, 0, j)),
            pl.BlockSpec((None, ATTN_KV_HEADS, bm, ATTN_HEAD_DIM), lambda i, j: (i, 0, j, 0)),
            pl.BlockSpec((None, nkv, bm), lambda i, j: (i, 0, j)),
        ],
        name="attn_qkv",
    )


def _attn_core_body(qT_ref, kp_ref, ko_ref, vTp_ref, vTo_ref, sink_ref, oT_ref):
    n = pl.program_id(1)
    qb, hd, ck, grp = ATTN_BLOCK, ATTN_HEAD_DIM, CHUNK, ATTN_GROUP
    nvis = 3 * ck
    low = lax.broadcasted_iota(jnp.int32, (hd, LANES), 1) < ck
    no_prev = jnp.where(n == 0, NEG, 0.0)
    zeros = jnp.zeros((ck, grp * ck), BF16)
    nchunk = qb // ck

    def scores(h, c):
        tiles = [qT_ref[(h * grp + g) * hd:(h * grp + g + 1) * hd, :] for g in range(grp)]
        if c == 0:
            pairs = [jnp.where(low, tiles[2 * j], pltpu.roll(tiles[2 * j + 1], ck, axis=1)) for j in range(grp // 2)]
        else:
            pairs = [jnp.where(low, pltpu.roll(tiles[2 * j], ck, axis=1), tiles[2 * j + 1]) for j in range(grp // 2)]
        q = jnp.concatenate(pairs, axis=1)
        kb = jnp.concatenate([kp_ref[h], ko_ref[h]], axis=0)
        return jnp.dot(kb[c * ck:c * ck + nvis, :], q, preferred_element_type=F32)

    def attend(h, c, s):
        nprev = qb - c * ck
        s = jnp.concatenate([s[:nprev] + no_prev, s[nprev:]], axis=0)
        sk = sink_ref[h]
        m = jnp.maximum(jnp.max(s, axis=0, keepdims=True), sk)
        p = jnp.exp2(s - m)
        l = jnp.sum(p, axis=0, keepdims=True) + jnp.exp2(sk - m)
        pb = p.astype(BF16)
        pb = jnp.concatenate([pb, zeros] if c == 0 else [zeros, pb], axis=0)
        vb = jnp.concatenate([vTp_ref[h * hd:(h + 1) * hd, :], vTo_ref[h * hd:(h + 1) * hd, :]], axis=1)
        return (jnp.dot(vb, pb, preferred_element_type=F32) / l).astype(BF16)

    subs = [(h, c) for h in range(ATTN_KV_HEADS) for c in range(nchunk)]
    ahead = ATTN_SCORE_LOOKAHEAD
    pending = [scores(*sub) for sub in subs[:ahead]]
    outs = {}
    for i, (h, c) in enumerate(subs):
        s_cur = pending.pop(0)
        if i + ahead < len(subs):
            pending.append(scores(*subs[i + ahead]))
        outs[c] = attend(h, c, s_cur)
        if c == nchunk - 1:
            for j in range(grp // 2):
                a0 = outs[0][:, j * LANES:(j + 1) * LANES]
                a1 = outs[1][:, j * LANES:(j + 1) * LANES]
                r0 = (h * grp + 2 * j) * hd
                oT_ref[r0:r0 + hd, :] = jnp.where(low, a0, pltpu.roll(a1, ck, axis=1))
                oT_ref[r0 + hd:r0 + 2 * hd, :] = jnp.where(low, pltpu.roll(a0, ck, axis=1), a1)


def _attn_core(qT, k, vT, sink_rows, cast=()):
    b, nq, s = qT.shape
    nkv = vT.shape[1]
    qb = ATTN_BLOCK
    prev = lambda j: jnp.maximum(j - 1, 0)
    (oT,), casted = _call_with_casts(
        _attn_core_body,
        (qT, k, k, vT, vT, sink_rows),
        cast,
        out_shape=[jax.ShapeDtypeStruct((b, nq, s), BF16)],
        grid=(b, s // qb),
        in_specs=[
            pl.BlockSpec((None, nq, qb), lambda i, j: (i, 0, j)),
            pl.BlockSpec((None, ATTN_KV_HEADS, qb, ATTN_HEAD_DIM), lambda i, j: (i, 0, prev(j), 0)),
            pl.BlockSpec((None, ATTN_KV_HEADS, qb, ATTN_HEAD_DIM), lambda i, j: (i, 0, j, 0)),
            pl.BlockSpec((None, nkv, qb), lambda i, j: (i, 0, prev(j))),
            pl.BlockSpec((None, nkv, qb), lambda i, j: (i, 0, j)),
            _resident((ATTN_KV_HEADS, 1, ATTN_GROUP * CHUNK)),
        ],
        out_specs=[pl.BlockSpec((None, nq, qb), lambda i, j: (i, 0, j))],
        name="attn_core",
    )
    return oT, casted


def _proj_res_t_body(aT_ref, w_ref, x_ref, o_ref):
    acc = lax.dot_general(aT_ref[...], w_ref[...], (((0,), (0,)), ((), ())), preferred_element_type=F32)
    o_ref[...] = x_ref[...] + acc


def _proj_res_t(aT, w, x3d, *, bm=512):
    b, k, s = aT.shape
    d = w.shape[1]
    return pl.pallas_call(
        _proj_res_t_body,
        out_shape=jax.ShapeDtypeStruct((b, s, d), F32),
        grid=(b, s // bm),
        in_specs=[
            pl.BlockSpec((None, k, bm), lambda i, j: (i, 0, j)),
            _resident((k, d)),
            pl.BlockSpec((None, bm, d), lambda i, j: (i, j, 0)),
        ],
        out_specs=pl.BlockSpec((None, bm, d), lambda i, j: (i, j, 0)),
        compiler_params=_params("parallel", "parallel"),
        name="proj_res_t",
    )(aT, w, x3d)


def _attention_layer(x3d, mix_norm, w_qkv, q_norm, k_norm, sinks, w_o, cast_qkv=(), cast_core=()):
    b, s, d = x3d.shape
    nq = ATTN_HEADS * ATTN_HEAD_DIM
    nkv = ATTN_KV_HEADS * ATTN_HEAD_DIM
    wq, wk, wv = w_qkv[:, :nq], w_qkv[:, nq:nq + nkv], w_qkv[:, nq + nkv:]
    wqvT = jnp.concatenate([wq, wv], axis=1).T.astype(BF16)
    wk_pad = jnp.pad(wk.reshape(d, ATTN_KV_HEADS, ATTN_HEAD_DIM),
                     ((0, 0), (0, 0), (0, LANES - ATTN_HEAD_DIM))).reshape(d, ATTN_KV_HEADS * LANES).astype(BF16)
    kgain = (q_norm * k_norm).astype(F32) * LOG2E
    sink_rows = jnp.repeat(sinks.astype(F32).reshape(ATTN_KV_HEADS, 1, ATTN_GROUP) * LOG2E, CHUNK, axis=2)
    if w_o.dtype != BF16:
        cast_qkv = (w_o,) + tuple(cast_qkv)
    (qT, k, vT), casted_qkv = _attn_qkv(x3d, mix_norm, wqvT, wk_pad, kgain, cast_qkv)
    if w_o.dtype != BF16:
        w_o, casted_qkv = casted_qkv[0], casted_qkv[1:]
    oT, casted_core = _attn_core(qT, k, vT, sink_rows, cast_core)
    return _proj_res_t(oT, w_o, x3d), casted_qkv, casted_core


def _conv_body(x_ref, g_ref, win_ref, bin_ref, dw_ref, dwb_ref, lng_ref, lnb_ref, wout_ref, bout_ref,
               o_ref, gbuf, cbuf, ybuf):
    bm, d = x_ref.shape
    nslab = d // LANES

    @pl.when(pl.program_id(1) == 0)
    def _():
        gbuf[:, 0:CONV_HALO, :] = jnp.zeros((nslab, CONV_HALO, LANES), F32)

    x = x_ref[...]
    h = _rms_norm_rows(x, g_ref[...]).astype(BF16)
    u = jnp.dot(h, win_ref[...], preferred_element_type=F32) + bin_ref[...]
    glu = u[:, :d] * jax.nn.sigmoid(u[:, d:])
    for j in range(nslab):
        gbuf[j, CONV_HALO:CONV_HALO + bm, :] = glu[:, j * LANES:(j + 1) * LANES]

    def slab(j, carry):
        acc = jnp.broadcast_to(dwb_ref[j], (bm, LANES))
        taps = dw_ref[j]
        for t in range(CONV_WIDTH):
            r0 = CONV_HALO - (CONV_WIDTH - 1) + t
            acc = acc + taps[t:t + 1, :] * gbuf[j, pl.ds(r0, bm), :]
        cbuf[j] = acc
        gbuf[j, 0:CONV_HALO, :] = gbuf[j, pl.ds(bm, CONV_HALO), :]
        return carry

    lax.fori_loop(0, nslab, slab, 0)

    tot = cbuf[0]
    for j in range(1, nslab):
        tot = tot + cbuf[j]
    mu = jnp.sum(tot, axis=-1, keepdims=True) * (1.0 / d)
    sq = None
    for j in range(nslab):
        c = cbuf[j] - mu
        sq = c * c if sq is None else sq + c * c
    inv = lax.rsqrt(jnp.sum(sq, axis=-1, keepdims=True) * (1.0 / d) + EPS)
    for j in range(nslab):
        y = (cbuf[j] - mu) * inv * lng_ref[:, j * LANES:(j + 1) * LANES] + lnb_ref[:, j * LANES:(j + 1) * LANES]
        ybuf[:, j * LANES:(j + 1) * LANES] = (y * jax.nn.sigmoid(y)).astype(BF16)
    o_ref[...] = x + jnp.dot(ybuf[...], wout_ref[...], preferred_element_type=F32) + bout_ref[...]


def _conv_layer(x3d, mix_norm, w_in, b_in, dw, dw_b, ln_g, ln_b, w_out, b_out, cast=(), *, bm=256):
    b, s, d = x3d.shape
    nslab = d // LANES
    dw_slabs = dw.astype(F32).reshape(CONV_WIDTH, nslab, LANES).transpose(1, 0, 2)
    dwb_slabs = dw_b.astype(F32).reshape(nslab, 1, LANES)
    row = lambda v: v.astype(F32).reshape(1, -1)
    (out,), casted = _call_with_casts(
        _conv_body,
        (x3d, row(mix_norm), w_in.astype(BF16), row(b_in), dw_slabs, dwb_slabs, row(ln_g), row(ln_b),
         w_out.astype(BF16), row(b_out)),
        cast,
        out_shape=[jax.ShapeDtypeStruct((b, s, d), F32)],
        grid=(b, s // bm),
        in_specs=[
            pl.BlockSpec((None, bm, d), lambda i, j: (i, j, 0)),
            _resident((1, d)),
            _resident((d, 2 * d)),
            _resident((1, 2 * d)),
            _resident((nslab, CONV_WIDTH, LANES)),
            _resident((nslab, 1, LANES)),
            _resident((1, d)),
            _resident((1, d)),
            _resident((d, d)),
            _resident((1, d)),
        ],
        out_specs=[pl.BlockSpec((None, bm, d), lambda i, j: (i, j, 0))],
        scratch_shapes=[
            pltpu.VMEM((nslab, CONV_HALO + bm, LANES), F32),
            pltpu.VMEM((nslab, bm, LANES), F32),
            pltpu.VMEM((bm, d), BF16),
        ],
        name="conv_module",
    )
    return out, casted


def _mlstm_proj_body(x_ref, g_ref, wq_ref, wk_ref, wv_ref, wo_ref, wi_ref, wf_ref, bo_ref, bi_ref, bf_ref,
                     q_ref, kT_ref, v_ref, og_ref, gi_ref, gf_ref):
    h = _rms_norm_rows(x_ref[...], g_ref[...]).astype(BF16)
    q = jnp.dot(h, wq_ref[...], preferred_element_type=F32)
    q_ref[...] = (q * (MLSTM_QK_DIM ** -0.5)).astype(BF16)
    kT_ref[...] = jnp.dot(h, wk_ref[...], preferred_element_type=F32).astype(BF16).T
    v_ref[...] = jnp.dot(h, wv_ref[...], preferred_element_type=F32).astype(BF16)
    o_pre = jnp.dot(h, wo_ref[...], preferred_element_type=F32) + bo_ref[...]
    og_ref[...] = jax.nn.sigmoid(o_pre).astype(BF16)
    gi_ref[...] = jnp.dot(h, wi_ref[...], preferred_element_type=F32) + bi_ref[...]
    gf_ref[...] = jnp.dot(h, wf_ref[...], preferred_element_type=F32) + bf_ref[...]


def _mlstm_proj(x3d, gain, w_in, wi, wf, bo, bi, bf, cast=(), *, bm=256):
    b, s, d = x3d.shape
    nqk = MLSTM_HEADS * MLSTM_QK_DIM
    tok = lambda n: pl.BlockSpec((None, bm, n), lambda i, j: (i, j, 0))
    cols = lambda n, c: pl.BlockSpec((d, n), lambda i, j: (0, c), pipeline_mode=pl.Buffered(1))
    assert d == 2 * nqk
    return _call_with_casts(
        _mlstm_proj_body,
        (x3d, gain.reshape(1, d), w_in, w_in, w_in, w_in, wi, wf, bo, bi, bf),
        cast,
        out_shape=[
            jax.ShapeDtypeStruct((b, s, nqk), BF16),
            jax.ShapeDtypeStruct((b, nqk, s), BF16),
            jax.ShapeDtypeStruct((b, s, d), BF16),
            jax.ShapeDtypeStruct((b, s, d), BF16),
            jax.ShapeDtypeStruct((b, s, LANES), F32),
            jax.ShapeDtypeStruct((b, s, LANES), F32),
        ],
        grid=(b, s // bm),
        in_specs=[
            tok(d),
            _resident((1, d)),
            cols(nqk, 0),
            cols(nqk, 1),
            cols(d, 1),
            cols(d, 2),
            _resident((d, LANES)),
            _resident((d, LANES)),
            _resident((1, d)),
            _resident((1, LANES)),
            _resident((1, LANES)),
        ],
        out_specs=[
            tok(nqk),
            pl.BlockSpec((None, nqk, bm), lambda i, j: (i, 0, j)),
            tok(d),
            tok(d),
            tok(LANES),
            tok(LANES),
        ],
        name="mlstm_proj",
    )


def _scan_rows(x, op, fill, period):
    r = lax.broadcasted_iota(jnp.int32, x.shape, 0) % period
    k = 1
    while k < period:
        x = op(x, jnp.where(r >= k, pltpu.roll(x, k, axis=0), fill))
        k *= 2
    return x


def _mlstm_gates_body(gi_ref, gf_ref, negm_ref, winter_ref, floor_ref, u_ref, kw_ref, dec_ref):
    s = gi_ref.shape[0]
    L = MLSTM_CHUNK
    nc = s // L
    gf = gf_ref[...]
    logf = jnp.minimum(gf, 0.0) - jnp.log(1.0 + jnp.exp(-jnp.abs(gf)))
    b = _scan_rows(logf, jnp.add, 0.0, L)
    u = gi_ref[...] - b
    cm = _scan_rows(u, jnp.maximum, -jnp.inf, L)
    m = jnp.zeros((1, LANES), F32)
    for c in range(nc):
        rows = slice(c * L, (c + 1) * L)
        cm_c = cm[rows]
        b_c = b[rows]
        u_c = u[rows]
        big_m = jnp.maximum(m, cm_c)
        m_last = big_m[L - 1:L, :]
        negm = -big_m
        wint = jnp.exp(m - big_m)
        flo = jnp.exp(-(b_c + big_m))
        for hh in range(MLSTM_HEADS):
            negm_ref[hh, rows, :] = jnp.broadcast_to(negm[:, hh:hh + 1], (L, LANES))
            winter_ref[hh, rows, :] = jnp.broadcast_to(wint[:, hh:hh + 1], (L, LANES))
            floor_ref[hh, rows, :] = jnp.broadcast_to(flo[:, hh:hh + 1], (L, LANES))
        uT = u_c.T
        mlT = jnp.broadcast_to(m_last, (L, LANES)).T
        mpT = jnp.broadcast_to(m, (L, LANES)).T
        u_ref[:, rows] = uT[:MLSTM_HEADS]
        kw_ref[:, rows] = jnp.exp(uT - mlT)[:MLSTM_HEADS]
        dec_ref[:, rows] = jnp.exp(mpT - mlT)[:MLSTM_HEADS]
        m = b_c[L - 1:L, :] + m_last


def _mlstm_gates(gi, gf):
    b, s, _ = gi.shape
    hcol = jax.ShapeDtypeStruct((b, MLSTM_HEADS, s, LANES), F32)
    hrow = jax.ShapeDtypeStruct((b, MLSTM_HEADS, s), F32)
    tok = pl.BlockSpec((None, s, LANES), lambda i: (i, 0, 0))
    col = pl.BlockSpec((None, MLSTM_HEADS, s, LANES), lambda i: (i, 0, 0, 0))
    rowspec = pl.BlockSpec((None, MLSTM_HEADS, s), lambda i: (i, 0, 0))
    return pl.pallas_call(
        _mlstm_gates_body,
        out_shape=(hcol, hcol, hcol, hrow, hrow, hrow),
        grid=(b,),
        in_specs=[tok, tok],
        out_specs=(col, col, col, rowspec, rowspec, rowspec),
        compiler_params=_params("parallel"),
        name="mlstm_gates",
    )(gi, gf)


def _mlstm_core_body(q_ref, kT_ref, v_ref, og_ref, hn_ref, negm_ref, winter_ref, floor_ref,
                     u_ref, kw_ref, dec_ref, o_ref):
    L = MLSTM_CHUNK
    dk, dv = MLSTM_QK_DIM, MLSTM_V_DIM
    nc = q_ref.shape[0] // L
    tril = lax.broadcasted_iota(jnp.int32, (L, L), 0) >= lax.broadcasted_iota(jnp.int32, (L, L), 1)

    def state_free(c):
        rows = slice(c * L, (c + 1) * L)
        kT = kT_ref[:, rows]
        qk = jnp.dot(q_ref[rows, :], kT, preferred_element_type=F32)
        kwT = kT.astype(F32) * kw_ref[:, rows]
        inc = jnp.dot(kwT.astype(BF16), v_ref[rows, :], preferred_element_type=F32)
        return qk, inc, jnp.sum(kwT, axis=-1, keepdims=True)

    cmat = jnp.zeros((dk, dv), F32)
    nvec = jnp.zeros((dk, LANES), F32)
    pending = [state_free(c) for c in range(min(MLSTM_LOOKAHEAD, nc))]
    for c in range(nc):
        rows = slice(c * L, (c + 1) * L)
        qk, inc, ksum = pending.pop(0)
        if c + MLSTM_LOOKAHEAD < nc:
            pending.append(state_free(c + MLSTM_LOOKAHEAD))
        q = q_ref[rows, :]
        wint = winter_ref[rows, :]
        inter = jnp.dot(q, cmat.astype(BF16), preferred_element_type=F32)
        qn_inter = jnp.dot(q, nvec.astype(BF16), preferred_element_type=F32)
        dmat = jnp.where(tril, jnp.exp(u_ref[:, rows] + negm_ref[rows, :]), 0.0)
        p = qk * dmat
        num = jnp.dot(p.astype(BF16), v_ref[rows, :], preferred_element_type=F32)
        num = num + jnp.concatenate([wint, wint], axis=1) * inter
        qn = jnp.sum(p, axis=-1, keepdims=True) + wint * qn_inter
        den = jnp.maximum(jnp.abs(qn), floor_ref[rows, :])
        r = 1.0 / den
        scale = r * lax.rsqrt(jnp.mean(num * num, axis=-1, keepdims=True) * r * r + EPS)
        y = num * jnp.concatenate([scale, scale], axis=1) * hn_ref[...]
        o_ref[rows, :] = (og_ref[rows, :].astype(F32) * y).astype(BF16)

        dec = dec_ref[:, rows]
        cmat = jnp.concatenate([dec, dec], axis=1) * cmat + inc
        nvec = dec * nvec + ksum


def _mlstm_core(q, kT, v, og, h_norm, negm, winter, floor, u, kw, dec):
    b, s, _ = q.shape
    dk, dv = MLSTM_QK_DIM, MLSTM_V_DIM
    d = MLSTM_HEADS * dv
    col = pl.BlockSpec((None, None, s, LANES), lambda i, j: (i, j, 0, 0))
    rowspec = pl.BlockSpec((None, None, 1, s), lambda i, j: (i, j, 0, 0))
    per_head_row = lambda a: a.reshape(b, MLSTM_HEADS, 1, s)
    return pl.pallas_call(
        _mlstm_core_body,
        out_shape=jax.ShapeDtypeStruct((b, s, d), BF16),
        grid=(b, MLSTM_HEADS),
        in_specs=[
            pl.BlockSpec((None, s, dk), lambda i, j: (i, 0, j)),
            pl.BlockSpec((None, dk, s), lambda i, j: (i, j, 0)),
            pl.BlockSpec((None, s, dv), lambda i, j: (i, 0, j)),
            pl.BlockSpec((None, s, dv), lambda i, j: (i, 0, j)),
            pl.BlockSpec((1, dv), lambda i, j: (0, j)),
            col, col, col, rowspec, rowspec, rowspec,
        ],
        out_specs=pl.BlockSpec((None, s, dv), lambda i, j: (i, 0, j)),
        compiler_params=_params("parallel", "parallel"),
        name="mlstm_core",
    )(q, kT, v, og, h_norm.reshape(1, d), negm, winter, floor, per_head_row(u), per_head_row(kw), per_head_row(dec))


def _mlstm_layer(x3d, mix_norm, w_in, w_in_bf16, b_gates, h_norm, w_out, cast=()):
    b, s, d = x3d.shape
    nh = MLSTM_HEADS
    nqk = nh * MLSTM_QK_DIM
    wg = w_in[:, 2 * nqk + 2 * d:]
    pad = lambda w: jnp.pad(w, ((0, 0), (0, LANES - nh))).astype(BF16)
    padb = lambda v: jnp.pad(v.astype(F32), (0, LANES - nh)).reshape(1, LANES)
    bo = b_gates[:d].astype(F32).reshape(1, d)
    (q, kT, v, og, gi, gf), (w_out_bf16, *casted) = _mlstm_proj(
        x3d, mix_norm, w_in_bf16, pad(wg[:, :nh]), pad(wg[:, nh:]),
        bo, padb(b_gates[d:d + nh]), padb(b_gates[d + nh:]), (w_out,) + tuple(cast))
    negm, winter, floor, u, kw, dec = _mlstm_gates(gi, gf)
    hgated = _mlstm_core(q, kT, v, og, h_norm.astype(F32), negm, winter, floor, u, kw, dec)
    out = _proj_res(hgated.reshape(b * s, d), w_out_bf16, jnp.zeros((d,), F32), x3d.reshape(b * s, d))
    return out.reshape(b, s, d), casted


def kernel(x, l0_mix_norm, l0_attn_w_qkv, l0_attn_q_norm, l0_attn_k_norm, l0_attn_sinks, l0_attn_w_o, l0_mlp_norm, l0_mlp_w1, l0_mlp_w2, l1_mix_norm, l1_conv_w_in, l1_conv_b_in, l1_conv_dw, l1_conv_dw_b, l1_conv_ln_g, l1_conv_ln_b, l1_conv_w_out, l1_conv_b_out, l1_mlp_norm, l1_mlp_w1, l1_mlp_w2, l2_mix_norm, l2_mlstm_w_in, l2_mlstm_b_gates, l2_mlstm_h_norm, l2_mlstm_w_out, l2_mlp_norm, l2_mlp_w1, l2_mlp_w2, l3_mix_norm, l3_attn_w_qkv, l3_attn_q_norm, l3_attn_k_norm, l3_attn_sinks, l3_attn_w_o, l3_mlp_norm, l3_mlp_w1, l3_mlp_w2):
    b, s, d = x.shape

    def mlp(x3d, norm, w1, w2):
        return _mlp(x3d.reshape(b * s, d), norm.astype(F32), w1, w2).reshape(b, s, d)

    x, (w1, w2), (conv_w_in, conv_w_out) = _attention_layer(
        x, l0_mix_norm, l0_attn_w_qkv, l0_attn_q_norm, l0_attn_k_norm, l0_attn_sinks, l0_attn_w_o,
        cast_qkv=(l0_mlp_w1, l0_mlp_w2), cast_core=(l1_conv_w_in, l1_conv_w_out))
    x = mlp(x, l0_mlp_norm, w1, w2)
    x, (mlstm_w_in, w1, w2) = _conv_layer(
        x, l1_mix_norm, conv_w_in, l1_conv_b_in, l1_conv_dw, l1_conv_dw_b, l1_conv_ln_g, l1_conv_ln_b, conv_w_out,
        l1_conv_b_out, cast=(l2_mlstm_w_in, l1_mlp_w1, l1_mlp_w2))
    x = mlp(x, l1_mlp_norm, w1, w2)
    x, (w1, w2) = _mlstm_layer(x, l2_mix_norm, l2_mlstm_w_in, mlstm_w_in, l2_mlstm_b_gates, l2_mlstm_h_norm,
                               l2_mlstm_w_out, cast=(l2_mlp_w1, l2_mlp_w2))
    x = mlp(x, l2_mlp_norm, w1, w2)
    x, (w1, w2), _ = _attention_layer(
        x, l3_mix_norm, l3_attn_w_qkv, l3_attn_q_norm, l3_attn_k_norm, l3_attn_sinks, l3_attn_w_o,
        cast_qkv=(l3_mlp_w1, l3_mlp_w2))
    x = mlp(x, l3_mlp_norm, w1, w2)
    return x
```

```python
import functools

import jax
import jax.numpy as jnp
from jax import lax
from jax.experimental import pallas as pl
from jax.experimental.pallas import tpu as pltpu

F32 = jnp.float32
BF16 = jnp.bfloat16

EPS = 1e-6
NEG = -1e30
LOG2E = 1.4426950408889634

ATTN_HEADS = 32
ATTN_KV_HEADS = 4
ATTN_GROUP = ATTN_HEADS // ATTN_KV_HEADS
ATTN_HEAD_DIM = 64
ATTN_BLOCK = 128
CHUNK = 64
CONV_WIDTH = 31
MLSTM_HEADS = 8
MLSTM_QK_DIM = 128
MLSTM_V_DIM = 256
MLSTM_CHUNK = 128

LANES = 128
VMEM_LIMIT_BYTES = 60000 * 1024

QKV_ROW_SPLIT = 2
ATTN_SCORE_LOOKAHEAD = 2
MLSTM_LOOKAHEAD = 1
CONV_HALO = 32


def _params(*sem):
    return pltpu.CompilerParams(dimension_semantics=sem, vmem_limit_bytes=VMEM_LIMIT_BYTES)


def _resident(shape):
    nd = len(shape)
    return pl.BlockSpec(shape, lambda *_: (0,) * nd, pipeline_mode=pl.Buffered(1))


def _rms_norm_rows(x, gain):
    return x * lax.rsqrt(jnp.mean(x * x, axis=-1, keepdims=True) + EPS) * gain


BF16_SUBLANES = 16


def _cast_specs(w, grid):
    rows, cols = w.shape
    steps = 1
    for g in grid:
        steps *= g
    per_step = rows // steps
    assert per_step * steps == rows and per_step >= 1
    blk = max(per_step, BF16_SUBLANES)
    share = blk // per_step

    def index_map(*g):
        flat = g[0]
        for n, i in zip(grid[1:], g[1:]):
            flat = flat * n + i
        return flat // share, 0

    return pl.BlockSpec((blk, cols), index_map)


def _call_with_casts(body, args, cast, *, grid, in_specs, out_specs, out_shape, scratch_shapes=(), name):
    n_in, n_out, n_cast = len(in_specs), len(out_specs), len(cast)

    def wrapped(*refs):
        cast_in = refs[n_in:n_in + n_cast]
        cast_out = refs[n_in + n_cast + n_out:n_in + 2 * n_cast + n_out]
        for src, dst in zip(cast_in, cast_out):
            dst[...] = src[...].astype(BF16)
        body(*refs[:n_in], *refs[n_in + n_cast:n_in + n_cast + n_out], *refs[n_in + 2 * n_cast + n_out:])

    side = [_cast_specs(w, grid) for w in cast]
    res = pl.pallas_call(
        wrapped,
        out_shape=list(out_shape) + [jax.ShapeDtypeStruct(w.shape, BF16) for w in cast],
        grid=grid,
        in_specs=list(in_specs) + side,
        out_specs=list(out_specs) + side,
        scratch_shapes=list(scratch_shapes),
        compiler_params=_params(*(("arbitrary",) * len(grid))),
        name=name,
    )(*args, *cast)
    return res[:n_out], res[n_out:]


def _mlp_body(x_ref, g_ref, w1_ref, w2_ref, o_ref, h_ref):
    def up_down(h):
        a = jnp.dot(h, w1_ref[...], preferred_element_type=F32)
        a = jnp.square(jnp.maximum(a, 0.0)).astype(BF16)
        return jnp.dot(a, w2_ref[...], preferred_element_type=F32)

    @pl.when(pl.program_id(1) == 0)
    def _():
        x = x_ref[...]
        h = _rms_norm_rows(x, g_ref[...]).astype(BF16)
        h_ref[...] = h
        o_ref[...] = x + up_down(h)

    @pl.when(pl.program_id(1) != 0)
    def _():
        o_ref[...] += up_down(h_ref[...])


def _mlp(x2d, gain, w1, w2, *, bm=1024, fc=1024):
    m, d = x2d.shape
    ff = w1.shape[1]
    return pl.pallas_call(
        _mlp_body,
        out_shape=jax.ShapeDtypeStruct((m, d), F32),
        grid=(m // bm, ff // fc),
        in_specs=[
            pl.BlockSpec((bm, d), lambda i, j: (i, 0)),
            pl.BlockSpec((1, d), lambda i, j: (0, 0)),
            pl.BlockSpec((d, fc), lambda i, j: (0, j)),
            pl.BlockSpec((fc, d), lambda i, j: (j, 0)),
        ],
        out_specs=pl.BlockSpec((bm, d), lambda i, j: (i, 0)),
        scratch_shapes=[pltpu.VMEM((bm, d), BF16)],
        compiler_params=_params("parallel", "arbitrary"),
        name="mlp",
    )(x2d, gain.reshape(1, d), w1, w2)


def _proj_res_body(a_ref, w_ref, b_ref, x_ref, o_ref):
    acc = jnp.dot(a_ref[...], w_ref[...], preferred_element_type=F32)
    o_ref[...] = x_ref[...] + acc + b_ref[...]


def _proj_res(a2d, w, bias, x2d, *, bm=512):
    m, k = a2d.shape
    d = w.shape[1]
    return pl.pallas_call(
        _proj_res_body,
        out_shape=jax.ShapeDtypeStruct((m, d), F32),
        grid=(m // bm,),
        in_specs=[
            pl.BlockSpec((bm, k), lambda i: (i, 0)),
            _resident((k, d)),
            _resident((1, d)),
            pl.BlockSpec((bm, d), lambda i: (i, 0)),
        ],
        out_specs=pl.BlockSpec((bm, d), lambda i: (i, 0)),
        compiler_params=_params("parallel"),
        name="proj_res",
    )(a2d, w, bias.reshape(1, d), x2d)


def _attn_qkv_body(x_ref, g_ref, wqvT_ref, wk_ref, kgain_ref, qT_ref, k_ref, vT_ref):
    nq = ATTN_HEADS * ATTN_HEAD_DIM
    bm = x_ref.shape[0]
    rt = bm // QKV_ROW_SPLIT
    prods = []
    for r in range(QKV_ROW_SPLIT):
        h = _rms_norm_rows(x_ref[r * rt:(r + 1) * rt, :], g_ref[...]).astype(BF16)
        qvT = lax.dot_general(wqvT_ref[...], h, (((1,), (1,)), ((), ())),
                              preferred_element_type=F32)
        kf = jnp.dot(h, wk_ref[...], preferred_element_type=F32)
        prods.append((qvT, kf))
    for r, (qvT, kf) in enumerate(prods):
        q3 = qvT[:nq].reshape(ATTN_HEADS, ATTN_HEAD_DIM, rt)
        inv = lax.rsqrt(jnp.mean(q3 * q3, axis=1, keepdims=True) + EPS) * (ATTN_HEAD_DIM ** -0.5)
        qT_ref[:, r * rt:(r + 1) * rt] = (q3 * inv).reshape(nq, rt).astype(BF16)
        vT_ref[:, r * rt:(r + 1) * rt] = qvT[nq:].astype(BF16)
        for j in range(ATTN_KV_HEADS):
            kj = kf[:, j * LANES:j * LANES + ATTN_HEAD_DIM]
            kn = kj * lax.rsqrt(jnp.mean(kj * kj, axis=-1, keepdims=True) + EPS)
            k_ref[j, r * rt:(r + 1) * rt, :] = (kn * kgain_ref[...]).astype(BF16)


def _attn_qkv(x3d, gain, wqvT, wk_pad, kgain, cast=(), *, bm=512):
    b, s, d = x3d.shape
    nq = ATTN_HEADS * ATTN_HEAD_DIM
    nkv = ATTN_KV_HEADS * ATTN_HEAD_DIM
    return _call_with_casts(
        _attn_qkv_body,
        (x3d, gain.reshape(1, d), wqvT, wk_pad, kgain.reshape(1, ATTN_HEAD_DIM)),
        cast,
        out_shape=[
            jax.ShapeDtypeStruct((b, nq, s), BF16),
            jax.ShapeDtypeStruct((b, ATTN_KV_HEADS, s, ATTN_HEAD_DIM), BF16),
            jax.ShapeDtypeStruct((b, nkv, s), BF16),
        ],
        grid=(b, s // bm),
        in_specs=[
            pl.BlockSpec((None, bm, d), lambda i, j: (i, j, 0)),
            _resident((1, d)),
            _resident((nq + nkv, d)),
            _resident((d, ATTN_KV_HEADS * LANES)),
            _resident((1, ATTN_HEAD_DIM)),
        ],
        out_specs=[
            pl.BlockSpec((None, nq, bm), lambda i, j: (i, 0, j)),
            pl.BlockSpec((None, ATTN_KV_HEADS, bm, ATTN_HEAD_DIM), lambda i, j: (i, 0, j, 0)),
            pl.BlockSpec((None, nkv, bm), lambda i, j: (i, 0, j)),
        ],
        name="attn_qkv",
    )


def _attn_core_body(qT_ref, kp_ref, ko_ref, vTp_ref, vTo_ref, sink_ref, wo_ref, x_ref, out_ref, o_scr,
                    *, blocks_per_seq, n_blocks):
    t = pl.program_id(0)
    n = jnp.minimum(t, n_blocks - 1) % blocks_per_seq

    @pl.when(t == 0)
    def _():
        o_scr[...] = jnp.zeros(o_scr.shape, BF16)

    qb, hd, ck, grp = ATTN_BLOCK, ATTN_HEAD_DIM, CHUNK, ATTN_GROUP
    nvis = 3 * ck
    low = lax.broadcasted_iota(jnp.int32, (hd, LANES), 1) < ck
    no_prev = jnp.where(n == 0, NEG, 0.0)
    zeros = jnp.zeros((ck, grp * ck), BF16)
    nchunk = qb // ck

    def scores(h, c):
        tiles = [qT_ref[(h * grp + g) * hd:(h * grp + g + 1) * hd, :] for g in range(grp)]
        if c == 0:
            pairs = [jnp.where(low, tiles[2 * j], pltpu.roll(tiles[2 * j + 1], ck, axis=1)) for j in range(grp // 2)]
        else:
            pairs = [jnp.where(low, pltpu.roll(tiles[2 * j], ck, axis=1), tiles[2 * j + 1]) for j in range(grp // 2)]
        q = jnp.concatenate(pairs, axis=1)
        kb = jnp.concatenate([kp_ref[h], ko_ref[h]], axis=0)
        return jnp.dot(kb[c * ck:c * ck + nvis, :], q, preferred_element_type=F32)

    def attend(h, c, s):
        nprev = qb - c * ck
        s = jnp.concatenate([s[:nprev] + no_prev, s[nprev:]], axis=0)
        sk = sink_ref[h]
        m = jnp.maximum(jnp.max(s, axis=0, keepdims=True), sk)
        p = jnp.exp2(s - m)
        l = jnp.sum(p, axis=0, keepdims=True) + jnp.exp2(sk - m)
        pb = p.astype(BF16)
        pb = jnp.concatenate([pb, zeros] if c == 0 else [zeros, pb], axis=0)
        vb = jnp.concatenate([vTp_ref[h * hd:(h + 1) * hd, :], vTo_ref[h * hd:(h + 1) * hd, :]], axis=1)
        return (jnp.dot(vb, pb, preferred_element_type=F32) / l).astype(BF16)

    subs = [(h, c) for h in range(ATTN_KV_HEADS) for c in range(nchunk)]
    d_model = wo_ref.shape[1]
    pw = d_model // len(subs)

    def project(i):
        cols = slice(i * pw, (i + 1) * pw)
        out_ref[:, cols] = x_ref[:, cols] + jnp.dot(o_scr[...], wo_ref[:, cols], preferred_element_type=F32)

    ahead = ATTN_SCORE_LOOKAHEAD
    pending = [scores(*sub) for sub in subs[:ahead]]
    outs = {}
    new_tiles = []
    for i, (h, c) in enumerate(subs):
        s_cur = pending.pop(0)
        if i + ahead < len(subs):
            pending.append(scores(*subs[i + ahead]))
        outs[c] = attend(h, c, s_cur)
        project(i)
        if c == nchunk - 1:
            for j in range(grp // 2):
                a0 = outs[0][:, j * LANES:(j + 1) * LANES]
                a1 = outs[1][:, j * LANES:(j + 1) * LANES]
                even = jnp.where(low, a0, pltpu.roll(a1, ck, axis=1))
                odd = jnp.where(low, pltpu.roll(a0, ck, axis=1), a1)
                new_tiles.append(jnp.concatenate([even, odd], axis=0).T)
    for p_idx, tile in enumerate(new_tiles):
        o_scr[:, p_idx * LANES:(p_idx + 1) * LANES] = tile


def _attn_core(qT, k, vT, sink_rows, w_o, x2d):
    b, nq, s = qT.shape
    nkv = vT.shape[1]
    d = w_o.shape[1]
    qb = ATTN_BLOCK
    nb = s // qb
    total = b * nb
    cur = lambda t: jnp.minimum(t, total - 1)
    seq = lambda t: cur(t) // nb
    blk = lambda t: cur(t) % nb
    prev = lambda t: jnp.maximum(blk(t) - 1, 0)
    done = lambda t: jnp.maximum(t - 1, 0)
    return pl.pallas_call(
        functools.partial(_attn_core_body, blocks_per_seq=nb, n_blocks=total),
        out_shape=jax.ShapeDtypeStruct((b * s, d), F32),
        grid=(total + 1,),
        in_specs=[
            pl.BlockSpec((None, nq, qb), lambda t: (seq(t), 0, blk(t))),
            pl.BlockSpec((None, ATTN_KV_HEADS, qb, ATTN_HEAD_DIM), lambda t: (seq(t), 0, prev(t), 0)),
            pl.BlockSpec((None, ATTN_KV_HEADS, qb, ATTN_HEAD_DIM), lambda t: (seq(t), 0, blk(t), 0)),
            pl.BlockSpec((None, nkv, qb), lambda t: (seq(t), 0, prev(t))),
            pl.BlockSpec((None, nkv, qb), lambda t: (seq(t), 0, blk(t))),
            _resident((ATTN_KV_HEADS, 1, ATTN_GROUP * CHUNK)),
            _resident((nq, d)),
            pl.BlockSpec((qb, d), lambda t: (done(t), 0)),
        ],
        out_specs=pl.BlockSpec((qb, d), lambda t: (done(t), 0)),
        scratch_shapes=[pltpu.VMEM((qb, nq), BF16)],
        compiler_params=_params("arbitrary"),
        name="attn_core",
    )(qT, k, k, vT, vT, sink_rows, w_o, x2d)


def _proj_res_t_body(aT_ref, w_ref, x_ref, o_ref):
    acc = lax.dot_general(aT_ref[...], w_ref[...], (((0,), (0,)), ((), ())), preferred_element_type=F32)
    o_ref[...] = x_ref[...] + acc


def _proj_res_t(aT, w, x3d, *, bm=512):
    b, k, s = aT.shape
    d = w.shape[1]
    return pl.pallas_call(
        _proj_res_t_body,
        out_shape=jax.ShapeDtypeStruct((b, s, d), F32),
        grid=(b, s // bm),
        in_specs=[
            pl.BlockSpec((None, k, bm), lambda i, j: (i, 0, j)),
            _resident((k, d)),
            pl.BlockSpec((None, bm, d), lambda i, j: (i, j, 0)),
        ],
        out_specs=pl.BlockSpec((None, bm, d), lambda i, j: (i, j, 0)),
        compiler_params=_params("parallel", "parallel"),
        name="proj_res_t",
    )(aT, w, x3d)


def _attention_layer(x3d, mix_norm, w_qkv, q_norm, k_norm, sinks, w_o, cast=()):
    b, s, d = x3d.shape
    nq = ATTN_HEADS * ATTN_HEAD_DIM
    nkv = ATTN_KV_HEADS * ATTN_HEAD_DIM
    wq, wk, wv = w_qkv[:, :nq], w_qkv[:, nq:nq + nkv], w_qkv[:, nq + nkv:]
    wqvT = jnp.concatenate([wq, wv], axis=1).T.astype(BF16)
    wk_pad = jnp.pad(wk.reshape(d, ATTN_KV_HEADS, ATTN_HEAD_DIM),
                     ((0, 0), (0, 0), (0, LANES - ATTN_HEAD_DIM))).reshape(d, ATTN_KV_HEADS * LANES).astype(BF16)
    kgain = (q_norm * k_norm).astype(F32) * LOG2E
    sink_rows = jnp.repeat(sinks.astype(F32).reshape(ATTN_KV_HEADS, 1, ATTN_GROUP) * LOG2E, CHUNK, axis=2)
    (qT, k, vT), (w_o_bf16, *casted) = _attn_qkv(x3d, mix_norm, wqvT, wk_pad, kgain, (w_o,) + tuple(cast))
    out = _attn_core(qT, k, vT, sink_rows, w_o_bf16, x3d.reshape(b * s, d))
    return out.reshape(b, s, d), casted


def _conv_body(x_ref, g_ref, win_ref, bin_ref, dw_ref, dwb_ref, lng_ref, lnb_ref, wout_ref, bout_ref,
               o_ref, gbuf, cbuf, ybuf):
    bm, d = x_ref.shape
    nslab = d // LANES

    @pl.when(pl.program_id(1) == 0)
    def _():
        gbuf[:, 0:CONV_HALO, :] = jnp.zeros((nslab, CONV_HALO, LANES), F32)

    x = x_ref[...]
    h = _rms_norm_rows(x, g_ref[...]).astype(BF16)
    u = jnp.dot(h, win_ref[...], preferred_element_type=F32) + bin_ref[...]
    glu = u[:, :d] * jax.nn.sigmoid(u[:, d:])
    for j in range(nslab):
        gbuf[j, CONV_HALO:CONV_HALO + bm, :] = glu[:, j * LANES:(j + 1) * LANES]

    def slab(j, carry):
        acc = jnp.broadcast_to(dwb_ref[j], (bm, LANES))
        taps = dw_ref[j]
        for t in range(CONV_WIDTH):
            r0 = CONV_HALO - (CONV_WIDTH - 1) + t
            acc = acc + taps[t:t + 1, :] * gbuf[j, pl.ds(r0, bm), :]
        cbuf[j] = acc
        gbuf[j, 0:CONV_HALO, :] = gbuf[j, pl.ds(bm, CONV_HALO), :]
        return carry

    lax.fori_loop(0, nslab, slab, 0)

    tot = cbuf[0]
    for j in range(1, nslab):
        tot = tot + cbuf[j]
    mu = jnp.sum(tot, axis=-1, keepdims=True) * (1.0 / d)
    sq = None
    for j in range(nslab):
        c = cbuf[j] - mu
        sq = c * c if sq is None else sq + c * c
    inv = lax.rsqrt(jnp.sum(sq, axis=-1, keepdims=True) * (1.0 / d) + EPS)
    for j in range(nslab):
        y = (cbuf[j] - mu) * inv * lng_ref[:, j * LANES:(j + 1) * LANES] + lnb_ref[:, j * LANES:(j + 1) * LANES]
        ybuf[:, j * LANES:(j + 1) * LANES] = (y * jax.nn.sigmoid(y)).astype(BF16)
    o_ref[...] = x + jnp.dot(ybuf[...], wout_ref[...], preferred_element_type=F32) + bout_ref[...]


def _conv_layer(x3d, mix_norm, w_in, b_in, dw, dw_b, ln_g, ln_b, w_out, b_out, cast=(), *, bm=256):
    b, s, d = x3d.shape
    nslab = d // LANES
    dw_slabs = dw.astype(F32).reshape(CONV_WIDTH, nslab, LANES).transpose(1, 0, 2)
    dwb_slabs = dw_b.astype(F32).reshape(nslab, 1, LANES)
    row = lambda v: v.astype(F32).reshape(1, -1)
    (out,), casted = _call_with_casts(
        _conv_body,
        (x3d, row(mix_norm), w_in.astype(BF16), row(b_in), dw_slabs, dwb_slabs, row(ln_g), row(ln_b),
         w_out.astype(BF16), row(b_out)),
        cast,
        out_shape=[jax.ShapeDtypeStruct((b, s, d), F32)],
        grid=(b, s // bm),
        in_specs=[
            pl.BlockSpec((None, bm, d), lambda i, j: (i, j, 0)),
            _resident((1, d)),
            _resident((d, 2 * d)),
            _resident((1, 2 * d)),
            _resident((nslab, CONV_WIDTH, LANES)),
            _resident((nslab, 1, LANES)),
            _resident((1, d)),
            _resident((1, d)),
            _resident((d, d)),
            _resident((1, d)),
        ],
        out_specs=[pl.BlockSpec((None, bm, d), lambda i, j: (i, j, 0))],
        scratch_shapes=[
            pltpu.VMEM((nslab, CONV_HALO + bm, LANES), F32),
            pltpu.VMEM((nslab, bm, LANES), F32),
            pltpu.VMEM((bm, d), BF16),
        ],
        name="conv_module",
    )
    return out, casted


def _mlstm_proj_body(x_ref, g_ref, wq_ref, wk_ref, wv_ref, wo_ref, wi_ref, wf_ref, bo_ref, bi_ref, bf_ref,
                     q_ref, kT_ref, v_ref, og_ref, gi_ref, gf_ref):
    h = _rms_norm_rows(x_ref[...], g_ref[...]).astype(BF16)
    q = jnp.dot(h, wq_ref[...], preferred_element_type=F32)
    q_ref[...] = (q * (MLSTM_QK_DIM ** -0.5)).astype(BF16)
    kT_ref[...] = jnp.dot(h, wk_ref[...], preferred_element_type=F32).astype(BF16).T
    v_ref[...] = jnp.dot(h, wv_ref[...], preferred_element_type=F32).astype(BF16)
    o_pre = jnp.dot(h, wo_ref[...], preferred_element_type=F32) + bo_ref[...]
    og_ref[...] = jax.nn.sigmoid(o_pre).astype(BF16)
    gi_ref[...] = jnp.dot(h, wi_ref[...], preferred_element_type=F32) + bi_ref[...]
    gf_ref[...] = jnp.dot(h, wf_ref[...], preferred_element_type=F32) + bf_ref[...]


def _mlstm_proj(x3d, gain, w_in, wi, wf, bo, bi, bf, cast=(), *, bm=256):
    b, s, d = x3d.shape
    nqk = MLSTM_HEADS * MLSTM_QK_DIM
    tok = lambda n: pl.BlockSpec((None, bm, n), lambda i, j: (i, j, 0))
    cols = lambda n, c: pl.BlockSpec((d, n), lambda i, j: (0, c), pipeline_mode=pl.Buffered(1))
    assert d == 2 * nqk
    return _call_with_casts(
        _mlstm_proj_body,
        (x3d, gain.reshape(1, d), w_in, w_in, w_in, w_in, wi, wf, bo, bi, bf),
        cast,
        out_shape=[
            jax.ShapeDtypeStruct((b, s, nqk), BF16),
            jax.ShapeDtypeStruct((b, nqk, s), BF16),
            jax.ShapeDtypeStruct((b, s, d), BF16),
            jax.ShapeDtypeStruct((b, s, d), BF16),
            jax.ShapeDtypeStruct((b, s, LANES), F32),
            jax.ShapeDtypeStruct((b, s, LANES), F32),
        ],
        grid=(b, s // bm),
        in_specs=[
            tok(d),
            _resident((1, d)),
            cols(nqk, 0),
            cols(nqk, 1),
            cols(d, 1),
            cols(d, 2),
            _resident((d, LANES)),
            _resident((d, LANES)),
            _resident((1, d)),
            _resident((1, LANES)),
            _resident((1, LANES)),
        ],
        out_specs=[
            tok(nqk),
            pl.BlockSpec((None, nqk, bm), lambda i, j: (i, 0, j)),
            tok(d),
            tok(d),
            tok(LANES),
            tok(LANES),
        ],
        name="mlstm_proj",
    )


def _scan_rows(x, op, fill, period):
    r = lax.broadcasted_iota(jnp.int32, x.shape, 0) % period
    k = 1
    while k < period:
        x = op(x, jnp.where(r >= k, pltpu.roll(x, k, axis=0), fill))
        k *= 2
    return x


def _mlstm_gates_body(gi_ref, gf_ref, negm_ref, winter_ref, floor_ref, u_ref, kw_ref, dec_ref):
    s = gi_ref.shape[0]
    L = MLSTM_CHUNK
    nc = s // L
    gf = gf_ref[...]
    logf = jnp.minimum(gf, 0.0) - jnp.log(1.0 + jnp.exp(-jnp.abs(gf)))
    b = _scan_rows(logf, jnp.add, 0.0, L)
    u = gi_ref[...] - b
    cm = _scan_rows(u, jnp.maximum, -jnp.inf, L)
    m = jnp.zeros((1, LANES), F32)
    for c in range(nc):
        rows = slice(c * L, (c + 1) * L)
        cm_c = cm[rows]
        b_c = b[rows]
        u_c = u[rows]
        big_m = jnp.maximum(m, cm_c)
        m_last = big_m[L - 1:L, :]
        negm = -big_m
        wint = jnp.exp(m - big_m)
        flo = jnp.exp(-(b_c + big_m))
        for hh in range(MLSTM_HEADS):
            negm_ref[hh, rows, :] = jnp.broadcast_to(negm[:, hh:hh + 1], (L, LANES))
            winter_ref[hh, rows, :] = jnp.broadcast_to(wint[:, hh:hh + 1], (L, LANES))
            floor_ref[hh, rows, :] = jnp.broadcast_to(flo[:, hh:hh + 1], (L, LANES))
        uT = u_c.T
        mlT = jnp.broadcast_to(m_last, (L, LANES)).T
        mpT = jnp.broadcast_to(m, (L, LANES)).T
        u_ref[:, rows] = uT[:MLSTM_HEADS]
        kw_ref[:, rows] = jnp.exp(uT - mlT)[:MLSTM_HEADS]
        dec_ref[:, rows] = jnp.exp(mpT - mlT)[:MLSTM_HEADS]
        m = b_c[L - 1:L, :] + m_last


def _mlstm_gates(gi, gf):
    b, s, _ = gi.shape
    hcol = jax.ShapeDtypeStruct((b, MLSTM_HEADS, s, LANES), F32)
    hrow = jax.ShapeDtypeStruct((b, MLSTM_HEADS, s), F32)
    tok = pl.BlockSpec((None, s, LANES), lambda i: (i, 0, 0))
    col = pl.BlockSpec((None, MLSTM_HEADS, s, LANES), lambda i: (i, 0, 0, 0))
    rowspec = pl.BlockSpec((None, MLSTM_HEADS, s), lambda i: (i, 0, 0))
    return pl.pallas_call(
        _mlstm_gates_body,
        out_shape=(hcol, hcol, hcol, hrow, hrow, hrow),
        grid=(b,),
        in_specs=[tok, tok],
        out_specs=(col, col, col, rowspec, rowspec, rowspec),
        compiler_params=_params("parallel"),
        name="mlstm_gates",
    )(gi, gf)


def _mlstm_core_body(q_ref, kT_ref, v_ref, og_ref, hn_ref, negm_ref, winter_ref, floor_ref,
                     u_ref, kw_ref, dec_ref, o_ref):
    L = MLSTM_CHUNK
    dk, dv = MLSTM_QK_DIM, MLSTM_V_DIM
    nc = q_ref.shape[0] // L
    tril = lax.broadcasted_iota(jnp.int32, (L, L), 0) >= lax.broadcasted_iota(jnp.int32, (L, L), 1)

    def state_free(c):
        rows = slice(c * L, (c + 1) * L)
        kT = kT_ref[:, rows]
        qk = jnp.dot(q_ref[rows, :], kT, preferred_element_type=F32)
        kwT = kT.astype(F32) * kw_ref[:, rows]
        inc = jnp.dot(kwT.astype(BF16), v_ref[rows, :], preferred_element_type=F32)
        return qk, inc, jnp.sum(kwT, axis=-1, keepdims=True)

    cmat = jnp.zeros((dk, dv), F32)
    nvec = jnp.zeros((dk, LANES), F32)
    pending = [state_free(c) for c in range(min(MLSTM_LOOKAHEAD, nc))]
    for c in range(nc):
        rows = slice(c * L, (c + 1) * L)
        qk, inc, ksum = pending.pop(0)
        if c + MLSTM_LOOKAHEAD < nc:
            pending.append(state_free(c + MLSTM_LOOKAHEAD))
        q = q_ref[rows, :]
        wint = winter_ref[rows, :]
        inter = jnp.dot(q, cmat.astype(BF16), preferred_element_type=F32)
        qn_inter = jnp.dot(q, nvec.astype(BF16), preferred_element_type=F32)
        dmat = jnp.where(tril, jnp.exp(u_ref[:, rows] + negm_ref[rows, :]), 0.0)
        p = qk * dmat
        num = jnp.dot(p.astype(BF16), v_ref[rows, :], preferred_element_type=F32)
        num = num + jnp.concatenate([wint, wint], axis=1) * inter
        qn = jnp.sum(p, axis=-1, keepdims=True) + wint * qn_inter
        den = jnp.maximum(jnp.abs(qn), floor_ref[rows, :])
        r = 1.0 / den
        scale = r * lax.rsqrt(jnp.mean(num * num, axis=-1, keepdims=True) * r * r + EPS)
        y = num * jnp.concatenate([scale, scale], axis=1) * hn_ref[...]
        o_ref[rows, :] = (og_ref[rows, :].astype(F32) * y).astype(BF16)

        dec = dec_ref[:, rows]
        cmat = jnp.concatenate([dec, dec], axis=1) * cmat + inc
        nvec = dec * nvec + ksum


def _mlstm_core(q, kT, v, og, h_norm, negm, winter, floor, u, kw, dec):
    b, s, _ = q.shape
    dk, dv = MLSTM_QK_DIM, MLSTM_V_DIM
    d = MLSTM_HEADS * dv
    col = pl.BlockSpec((None, None, s, LANES), lambda i, j: (i, j, 0, 0))
    rowspec = pl.BlockSpec((None, None, 1, s), lambda i, j: (i, j, 0, 0))
    per_head_row = lambda a: a.reshape(b, MLSTM_HEADS, 1, s)
    return pl.pallas_call(
        _mlstm_core_body,
        out_shape=jax.ShapeDtypeStruct((b, s, d), BF16),
        grid=(b, MLSTM_HEADS),
        in_specs=[
            pl.BlockSpec((None, s, dk), lambda i, j: (i, 0, j)),
            pl.BlockSpec((None, dk, s), lambda i, j: (i, j, 0)),
            pl.BlockSpec((None, s, dv), lambda i, j: (i, 0, j)),
            pl.BlockSpec((None, s, dv), lambda i, j: (i, 0, j)),
            pl.BlockSpec((1, dv), lambda i, j: (0, j)),
            col, col, col, rowspec, rowspec, rowspec,
        ],
        out_specs=pl.BlockSpec((None, s, dv), lambda i, j: (i, 0, j)),
        compiler_params=_params("parallel", "parallel"),
        name="mlstm_core",
    )(q, kT, v, og, h_norm.reshape(1, d), negm, winter, floor, per_head_row(u), per_head_row(kw), per_head_row(dec))


def _mlstm_layer(x3d, mix_norm, w_in, b_gates, h_norm, w_out, cast=()):
    b, s, d = x3d.shape
    nh = MLSTM_HEADS
    nqk = nh * MLSTM_QK_DIM
    w_in_bf16 = w_in[:, :2 * nqk + 2 * d].astype(BF16)
    wg = w_in[:, 2 * nqk + 2 * d:]
    pad = lambda w: jnp.pad(w, ((0, 0), (0, LANES - nh))).astype(BF16)
    padb = lambda v: jnp.pad(v.astype(F32), (0, LANES - nh)).reshape(1, LANES)
    bo = b_gates[:d].astype(F32).reshape(1, d)
    (q, kT, v, og, gi, gf), (w_out_bf16, *casted) = _mlstm_proj(
        x3d, mix_norm, w_in_bf16, pad(wg[:, :nh]), pad(wg[:, nh:]),
        bo, padb(b_gates[d:d + nh]), padb(b_gates[d + nh:]), (w_out,) + tuple(cast))
    negm, winter, floor, u, kw, dec = _mlstm_gates(gi, gf)
    hgated = _mlstm_core(q, kT, v, og, h_norm.astype(F32), negm, winter, floor, u, kw, dec)
    out = _proj_res(hgated.reshape(b * s, d), w_out_bf16, jnp.zeros((d,), F32), x3d.reshape(b * s, d))
    return out.reshape(b, s, d), casted


def kernel(x, l0_mix_norm, l0_attn_w_qkv, l0_attn_q_norm, l0_attn_k_norm, l0_attn_sinks, l0_attn_w_o, l0_mlp_norm, l0_mlp_w1, l0_mlp_w2, l1_mix_norm, l1_conv_w_in, l1_conv_b_in, l1_conv_dw, l1_conv_dw_b, l1_conv_ln_g, l1_conv_ln_b, l1_conv_w_out, l1_conv_b_out, l1_mlp_norm, l1_mlp_w1, l1_mlp_w2, l2_mix_norm, l2_mlstm_w_in, l2_mlstm_b_gates, l2_mlstm_h_norm, l2_mlstm_w_out, l2_mlp_norm, l2_mlp_w1, l2_mlp_w2, l3_mix_norm, l3_attn_w_qkv, l3_attn_q_norm, l3_attn_k_norm, l3_attn_sinks, l3_attn_w_o, l3_mlp_norm, l3_mlp_w1, l3_mlp_w2):
    b, s, d = x.shape

    def mlp(x3d, norm, w1, w2):
        return _mlp(x3d.reshape(b * s, d), norm.astype(F32), w1, w2).reshape(b, s, d)

    x, (w1, w2, conv_w_in, conv_w_out) = _attention_layer(
        x, l0_mix_norm, l0_attn_w_qkv, l0_attn_q_norm, l0_attn_k_norm, l0_attn_sinks, l0_attn_w_o,
        cast=(l0_mlp_w1, l0_mlp_w2, l1_conv_w_in, l1_conv_w_out))
    x = mlp(x, l0_mlp_norm, w1, w2)
    x, (w1, w2) = _conv_layer(
        x, l1_mix_norm, conv_w_in, l1_conv_b_in, l1_conv_dw, l1_conv_dw_b, l1_conv_ln_g, l1_conv_ln_b, conv_w_out,
        l1_conv_b_out, cast=(l1_mlp_w1, l1_mlp_w2))
    x = mlp(x, l1_mlp_norm, w1, w2)
    x, (w1, w2) = _mlstm_layer(x, l2_mix_norm, l2_mlstm_w_in, l2_mlstm_b_gates, l2_mlstm_h_norm, l2_mlstm_w_out,
                               cast=(l2_mlp_w1, l2_mlp_w2))
    x = mlp(x, l2_mlp_norm, w1, w2)
    x, (w1, w2) = _attention_layer(
        x, l3_mix_norm, l3_attn_w_qkv, l3_attn_q_norm, l3_attn_k_norm, l3_attn_sinks, l3_attn_w_o,
        cast=(l3_mlp_w1, l3_mlp_w2))
    x = mlp(x, l3_mlp_norm, w1, w2)
    return x
```

```python
import functools

import jax
import jax.numpy as jnp
from jax import lax
from jax.experimental import pallas as pl
from jax.experimental.pallas import tpu as pltpu

F32 = jnp.float32
BF16 = jnp.bfloat16

EPS = 1e-6
NEG = -1e30
LOG2E = 1.4426950408889634

ATTN_HEADS = 32
ATTN_KV_HEADS = 4
ATTN_GROUP = ATTN_HEADS // ATTN_KV_HEADS
ATTN_HEAD_DIM = 64
ATTN_BLOCK = 128
CHUNK = 64
CONV_WIDTH = 31
MLSTM_HEADS = 8
MLSTM_QK_DIM = 128
MLSTM_V_DIM = 256
MLSTM_CHUNK = 128

LANES = 128
VMEM_LIMIT_BYTES = 60000 * 1024

QKV_ROW_SPLIT = 2
ATTN_STEP_BLOCKS = 2
ATTN_SCORE_LOOKAHEAD = 2
MLSTM_LOOKAHEAD = 1
CONV_HALO = 32


def _params(*sem):
    return pltpu.CompilerParams(dimension_semantics=sem, vmem_limit_bytes=VMEM_LIMIT_BYTES)


def _resident(shape):
    nd = len(shape)
    return pl.BlockSpec(shape, lambda *_: (0,) * nd, pipeline_mode=pl.Buffered(1))


def _rms_norm_rows(x, gain):
    return x * lax.rsqrt(jnp.mean(x * x, axis=-1, keepdims=True) + EPS) * gain


BF16_SUBLANES = 16


def _cast_specs(w, grid):
    rows, cols = w.shape
    steps = 1
    for g in grid:
        steps *= g
    per_step = rows // steps
    assert per_step * steps == rows and per_step >= 1
    blk = max(per_step, BF16_SUBLANES)
    share = blk // per_step

    def index_map(*g):
        flat = g[0]
        for n, i in zip(grid[1:], g[1:]):
            flat = flat * n + i
        return flat // share, 0

    return pl.BlockSpec((blk, cols), index_map)


def _call_with_casts(body, args, cast, *, grid, in_specs, out_specs, out_shape, scratch_shapes=(), name):
    n_in, n_out, n_cast = len(in_specs), len(out_specs), len(cast)

    def wrapped(*refs):
        cast_in = refs[n_in:n_in + n_cast]
        cast_out = refs[n_in + n_cast + n_out:n_in + 2 * n_cast + n_out]
        for src, dst in zip(cast_in, cast_out):
            dst[...] = src[...].astype(BF16)
        body(*refs[:n_in], *refs[n_in + n_cast:n_in + n_cast + n_out], *refs[n_in + 2 * n_cast + n_out:])

    side = [_cast_specs(w, grid) for w in cast]
    res = pl.pallas_call(
        wrapped,
        out_shape=list(out_shape) + [jax.ShapeDtypeStruct(w.shape, BF16) for w in cast],
        grid=grid,
        in_specs=list(in_specs) + side,
        out_specs=list(out_specs) + side,
        scratch_shapes=list(scratch_shapes),
        compiler_params=_params(*(("arbitrary",) * len(grid))),
        name=name,
    )(*args, *cast)
    return res[:n_out], res[n_out:]


def _mlp_body(x_ref, g_ref, w1_ref, w2_ref, o_ref, h_ref):
    def up_down(h):
        a = jnp.dot(h, w1_ref[...], preferred_element_type=F32)
        a = jnp.square(jnp.maximum(a, 0.0)).astype(BF16)
        return jnp.dot(a, w2_ref[...], preferred_element_type=F32)

    @pl.when(pl.program_id(1) == 0)
    def _():
        x = x_ref[...]
        h = _rms_norm_rows(x, g_ref[...]).astype(BF16)
        h_ref[...] = h
        o_ref[...] = x + up_down(h)

    @pl.when(pl.program_id(1) != 0)
    def _():
        o_ref[...] += up_down(h_ref[...])


def _mlp(x2d, gain, w1, w2, *, bm=1024, fc=1024):
    m, d = x2d.shape
    ff = w1.shape[1]
    return pl.pallas_call(
        _mlp_body,
        out_shape=jax.ShapeDtypeStruct((m, d), F32),
        grid=(m // bm, ff // fc),
        in_specs=[
            pl.BlockSpec((bm, d), lambda i, j: (i, 0)),
            pl.BlockSpec((1, d), lambda i, j: (0, 0)),
            pl.BlockSpec((d, fc), lambda i, j: (0, j)),
            pl.BlockSpec((fc, d), lambda i, j: (j, 0)),
        ],
        out_specs=pl.BlockSpec((bm, d), lambda i, j: (i, 0)),
        scratch_shapes=[pltpu.VMEM((bm, d), BF16)],
        compiler_params=_params("parallel", "arbitrary"),
        name="mlp",
    )(x2d, gain.reshape(1, d), w1, w2)


def _proj_res_body(a_ref, w_ref, b_ref, x_ref, o_ref):
    acc = jnp.dot(a_ref[...], w_ref[...], preferred_element_type=F32)
    o_ref[...] = x_ref[...] + acc + b_ref[...]


def _proj_res(a2d, w, bias, x2d, *, bm=512):
    m, k = a2d.shape
    d = w.shape[1]
    return pl.pallas_call(
        _proj_res_body,
        out_shape=jax.ShapeDtypeStruct((m, d), F32),
        grid=(m // bm,),
        in_specs=[
            pl.BlockSpec((bm, k), lambda i: (i, 0)),
            _resident((k, d)),
            _resident((1, d)),
            pl.BlockSpec((bm, d), lambda i: (i, 0)),
        ],
        out_specs=pl.BlockSpec((bm, d), lambda i: (i, 0)),
        compiler_params=_params("parallel"),
        name="proj_res",
    )(a2d, w, bias.reshape(1, d), x2d)


def _attn_qkv_body(x_ref, g_ref, wqvT_ref, wk_ref, kgain_ref, qT_ref, k_ref, vT_ref):
    nq = ATTN_HEADS * ATTN_HEAD_DIM
    bm = x_ref.shape[0]
    rt = bm // QKV_ROW_SPLIT
    prods = []
    for r in range(QKV_ROW_SPLIT):
        h = _rms_norm_rows(x_ref[r * rt:(r + 1) * rt, :], g_ref[...]).astype(BF16)
        qvT = lax.dot_general(wqvT_ref[...], h, (((1,), (1,)), ((), ())),
                              preferred_element_type=F32)
        kf = jnp.dot(h, wk_ref[...], preferred_element_type=F32)
        prods.append((qvT, kf))
    for r, (qvT, kf) in enumerate(prods):
        q3 = qvT[:nq].reshape(ATTN_HEADS, ATTN_HEAD_DIM, rt)
        inv = lax.rsqrt(jnp.mean(q3 * q3, axis=1, keepdims=True) + EPS) * (ATTN_HEAD_DIM ** -0.5)
        qT_ref[:, r * rt:(r + 1) * rt] = (q3 * inv).reshape(nq, rt).astype(BF16)
        vT_ref[:, r * rt:(r + 1) * rt] = qvT[nq:].astype(BF16)
        for j in range(ATTN_KV_HEADS):
            kj = kf[:, j * LANES:j * LANES + ATTN_HEAD_DIM]
            kn = kj * lax.rsqrt(jnp.mean(kj * kj, axis=-1, keepdims=True) + EPS)
            k_ref[j, r * rt:(r + 1) * rt, :] = (kn * kgain_ref[...]).astype(BF16)


def _attn_qkv(x3d, gain, wqvT, wk_pad, kgain, cast=(), *, bm=512):
    b, s, d = x3d.shape
    nq = ATTN_HEADS * ATTN_HEAD_DIM
    nkv = ATTN_KV_HEADS * ATTN_HEAD_DIM
    return _call_with_casts(
        _attn_qkv_body,
        (x3d, gain.reshape(1, d), wqvT, wk_pad, kgain.reshape(1, ATTN_HEAD_DIM)),
        cast,
        out_shape=[
            jax.ShapeDtypeStruct((b, nq, s), BF16),
            jax.ShapeDtypeStruct((b, ATTN_KV_HEADS, s, ATTN_HEAD_DIM), BF16),
            jax.ShapeDtypeStruct((b, nkv, s), BF16),
        ],
        grid=(b, s // bm),
        in_specs=[
            pl.BlockSpec((None, bm, d), lambda i, j: (i, j, 0)),
            _resident((1, d)),
            _resident((nq + nkv, d)),
            _resident((d, ATTN_KV_HEADS * LANES)),
            _resident((1, ATTN_HEAD_DIM)),
        ],
        out_specs=[
            pl.BlockSpec((None, nq, bm), lambda i, j: (i, 0, j)),
            pl.BlockSpec((None, ATTN_KV_HEADS, bm, ATTN_HEAD_DIM), lambda i, j: (i, 0, j, 0)),
            pl.BlockSpec((None, nkv, bm), lambda i, j: (i, 0, j)),
        ],
        name="attn_qkv",
    )


def _attn_core_body(qT_ref, kp_ref, ko_ref, vTp_ref, vTo_ref, sink_ref, wo_ref, x_ref, out_ref, o_scr,
                    *, steps_per_seq, n_steps):
    t = pl.program_id(0)
    first_of_seq = jnp.minimum(t, n_steps - 1) % steps_per_seq == 0

    @pl.when(t == 0)
    def _():
        o_scr[...] = jnp.zeros(o_scr.shape, BF16)

    qb, hd, ck, grp = ATTN_BLOCK, ATTN_HEAD_DIM, CHUNK, ATTN_GROUP
    nvis = 3 * ck
    low = lax.broadcasted_iota(jnp.int32, (hd, LANES), 1) < ck
    no_prev = jnp.where(first_of_seq, NEG, 0.0)
    zeros = jnp.zeros((ck, grp * ck), BF16)
    nchunk = qb // ck

    def band(a, h):
        rows = slice(h * hd, (h + 1) * hd)
        own = slice(a * qb, (a + 1) * qb)
        if a == 0:
            k_prev, v_prev = kp_ref[h], vTp_ref[rows, :]
        else:
            k_prev, v_prev = ko_ref[h, (a - 1) * qb:a * qb, :], vTo_ref[rows, (a - 1) * qb:a * qb]
        return (jnp.concatenate([k_prev, ko_ref[h, own, :]], axis=0),
                jnp.concatenate([v_prev, vTo_ref[rows, own]], axis=1))

    def scores(a, h, c):
        tiles = [qT_ref[(h * grp + g) * hd:(h * grp + g + 1) * hd, a * qb:(a + 1) * qb] for g in range(grp)]
        if c == 0:
            pairs = [jnp.where(low, tiles[2 * j], pltpu.roll(tiles[2 * j + 1], ck, axis=1)) for j in range(grp // 2)]
        else:
            pairs = [jnp.where(low, pltpu.roll(tiles[2 * j], ck, axis=1), tiles[2 * j + 1]) for j in range(grp // 2)]
        q = jnp.concatenate(pairs, axis=1)
        kb, _ = band(a, h)
        return jnp.dot(kb[c * ck:c * ck + nvis, :], q, preferred_element_type=F32)

    def attend(a, h, c, s):
        if a == 0:
            nprev = qb - c * ck
            s = jnp.concatenate([s[:nprev] + no_prev, s[nprev:]], axis=0)
        sk = sink_ref[h]
        m = jnp.maximum(jnp.max(s, axis=0, keepdims=True), sk)
        p = jnp.exp2(s - m)
        l = jnp.sum(p, axis=0, keepdims=True) + jnp.exp2(sk - m)
        pb = p.astype(BF16)
        pb = jnp.concatenate([pb, zeros] if c == 0 else [zeros, pb], axis=0)
        _, vb = band(a, h)
        return (jnp.dot(vb, pb, preferred_element_type=F32) / l).astype(BF16)

    subs = [(a, h, c) for a in range(ATTN_STEP_BLOCKS) for h in range(ATTN_KV_HEADS) for c in range(nchunk)]
    d_model = wo_ref.shape[1]
    n_proj = len(subs) // ATTN_STEP_BLOCKS
    pw = d_model // n_proj

    def project(i):
        cols = slice(i * pw, (i + 1) * pw)
        out_ref[:, cols] = x_ref[:, cols] + jnp.dot(o_scr[...], wo_ref[:, cols], preferred_element_type=F32)

    ahead = ATTN_SCORE_LOOKAHEAD
    pending = [scores(*sub) for sub in subs[:ahead]]
    outs = {}
    new_tiles = []
    for i, (a, h, c) in enumerate(subs):
        s_cur = pending.pop(0)
        if i + ahead < len(subs):
            pending.append(scores(*subs[i + ahead]))
        outs[c] = attend(a, h, c, s_cur)
        if i % ATTN_STEP_BLOCKS == ATTN_STEP_BLOCKS - 1:
            project(i // ATTN_STEP_BLOCKS)
        if c == nchunk - 1:
            for j in range(grp // 2):
                a0 = outs[0][:, j * LANES:(j + 1) * LANES]
                a1 = outs[1][:, j * LANES:(j + 1) * LANES]
                even = jnp.where(low, a0, pltpu.roll(a1, ck, axis=1))
                odd = jnp.where(low, pltpu.roll(a0, ck, axis=1), a1)
                new_tiles.append((a, h * (grp // 2) + j, jnp.concatenate([even, odd], axis=0).T))
    for a, p_idx, tile in new_tiles:
        o_scr[a * qb:(a + 1) * qb, p_idx * LANES:(p_idx + 1) * LANES] = tile


def _attn_core(qT, k, vT, sink_rows, w_o, x2d):
    b, nq, s = qT.shape
    nkv = vT.shape[1]
    d = w_o.shape[1]
    qb = ATTN_BLOCK
    sb = ATTN_STEP_BLOCKS * qb
    per_seq = s // sb
    total = b * per_seq
    cur = lambda t: jnp.minimum(t, total - 1)
    seq = lambda t: cur(t) // per_seq
    grp_i = lambda t: cur(t) % per_seq
    prev = lambda t: jnp.maximum(grp_i(t) * ATTN_STEP_BLOCKS - 1, 0)
    done = lambda t: jnp.maximum(t - 1, 0)
    return pl.pallas_call(
        functools.partial(_attn_core_body, steps_per_seq=per_seq, n_steps=total),
        out_shape=jax.ShapeDtypeStruct((b * s, d), F32),
        grid=(total + 1,),
        in_specs=[
            pl.BlockSpec((None, nq, sb), lambda t: (seq(t), 0, grp_i(t))),
            pl.BlockSpec((None, ATTN_KV_HEADS, qb, ATTN_HEAD_DIM), lambda t: (seq(t), 0, prev(t), 0)),
            pl.BlockSpec((None, ATTN_KV_HEADS, sb, ATTN_HEAD_DIM), lambda t: (seq(t), 0, grp_i(t), 0)),
            pl.BlockSpec((None, nkv, qb), lambda t: (seq(t), 0, prev(t))),
            pl.BlockSpec((None, nkv, sb), lambda t: (seq(t), 0, grp_i(t))),
            _resident((ATTN_KV_HEADS, 1, ATTN_GROUP * CHUNK)),
            _resident((nq, d)),
            pl.BlockSpec((sb, d), lambda t: (done(t), 0)),
        ],
        out_specs=pl.BlockSpec((sb, d), lambda t: (done(t), 0)),
        scratch_shapes=[pltpu.VMEM((sb, nq), BF16)],
        compiler_params=_params("arbitrary"),
        name="attn_core",
    )(qT, k, k, vT, vT, sink_rows, w_o, x2d)


def _attention_layer(x3d, mix_norm, w_qkv, q_norm, k_norm, sinks, w_o, cast=()):
    b, s, d = x3d.shape
    nq = ATTN_HEADS * ATTN_HEAD_DIM
    nkv = ATTN_KV_HEADS * ATTN_HEAD_DIM
    wq, wk, wv = w_qkv[:, :nq], w_qkv[:, nq:nq + nkv], w_qkv[:, nq + nkv:]
    wqvT = jnp.concatenate([wq, wv], axis=1).T.astype(BF16)
    wk_pad = jnp.pad(wk.reshape(d, ATTN_KV_HEADS, ATTN_HEAD_DIM),
                     ((0, 0), (0, 0), (0, LANES - ATTN_HEAD_DIM))).reshape(d, ATTN_KV_HEADS * LANES).astype(BF16)
    kgain = (q_norm * k_norm).astype(F32) * LOG2E
    sink_rows = jnp.repeat(sinks.astype(F32).reshape(ATTN_KV_HEADS, 1, ATTN_GROUP) * LOG2E, CHUNK, axis=2)
    (qT, k, vT), (w_o_bf16, *casted) = _attn_qkv(x3d, mix_norm, wqvT, wk_pad, kgain, (w_o,) + tuple(cast))
    out = _attn_core(qT, k, vT, sink_rows, w_o_bf16, x3d.reshape(b * s, d))
    return out.reshape(b, s, d), casted


def _conv_body(x_ref, g_ref, win_ref, bin_ref, dw_ref, dwb_ref, lng_ref, lnb_ref, wout_ref, bout_ref,
               o_ref, gbuf, cbuf, ybuf):
    bm, d = x_ref.shape
    nslab = d // LANES

    @pl.when(pl.program_id(1) == 0)
    def _():
        gbuf[:, 0:CONV_HALO, :] = jnp.zeros((nslab, CONV_HALO, LANES), F32)

    x = x_ref[...]
    h = _rms_norm_rows(x, g_ref[...]).astype(BF16)
    u = jnp.dot(h, win_ref[...], preferred_element_type=F32) + bin_ref[...]
    glu = u[:, :d] * jax.nn.sigmoid(u[:, d:])
    for j in range(nslab):
        gbuf[j, CONV_HALO:CONV_HALO + bm, :] = glu[:, j * LANES:(j + 1) * LANES]

    def slab(j, carry):
        acc = jnp.broadcast_to(dwb_ref[j], (bm, LANES))
        taps = dw_ref[j]
        for t in range(CONV_WIDTH):
            r0 = CONV_HALO - (CONV_WIDTH - 1) + t
            acc = acc + taps[t:t + 1, :] * gbuf[j, pl.ds(r0, bm), :]
        cbuf[j] = acc
        gbuf[j, 0:CONV_HALO, :] = gbuf[j, pl.ds(bm, CONV_HALO), :]
        return carry

    lax.fori_loop(0, nslab, slab, 0)

    tot = cbuf[0]
    for j in range(1, nslab):
        tot = tot + cbuf[j]
    mu = jnp.sum(tot, axis=-1, keepdims=True) * (1.0 / d)
    sq = None
    for j in range(nslab):
        c = cbuf[j] - mu
        sq = c * c if sq is None else sq + c * c
    inv = lax.rsqrt(jnp.sum(sq, axis=-1, keepdims=True) * (1.0 / d) + EPS)
    for j in range(nslab):
        y = (cbuf[j] - mu) * inv * lng_ref[:, j * LANES:(j + 1) * LANES] + lnb_ref[:, j * LANES:(j + 1) * LANES]
        ybuf[:, j * LANES:(j + 1) * LANES] = (y * jax.nn.sigmoid(y)).astype(BF16)
    o_ref[...] = x + jnp.dot(ybuf[...], wout_ref[...], preferred_element_type=F32) + bout_ref[...]


def _conv_layer(x3d, mix_norm, w_in, b_in, dw, dw_b, ln_g, ln_b, w_out, b_out, cast=(), *, bm=256):
    b, s, d = x3d.shape
    nslab = d // LANES
    dw_slabs = dw.astype(F32).reshape(CONV_WIDTH, nslab, LANES).transpose(1, 0, 2)
    dwb_slabs = dw_b.astype(F32).reshape(nslab, 1, LANES)
    row = lambda v: v.astype(F32).reshape(1, -1)
    (out,), casted = _call_with_casts(
        _conv_body,
        (x3d, row(mix_norm), w_in.astype(BF16), row(b_in), dw_slabs, dwb_slabs, row(ln_g), row(ln_b),
         w_out.astype(BF16), row(b_out)),
        cast,
        out_shape=[jax.ShapeDtypeStruct((b, s, d), F32)],
        grid=(b, s // bm),
        in_specs=[
            pl.BlockSpec((None, bm, d), lambda i, j: (i, j, 0)),
            _resident((1, d)),
            _resident((d, 2 * d)),
            _resident((1, 2 * d)),
            _resident((nslab, CONV_WIDTH, LANES)),
            _resident((nslab, 1, LANES)),
            _resident((1, d)),
            _resident((1, d)),
            _resident((d, d)),
            _resident((1, d)),
        ],
        out_specs=[pl.BlockSpec((None, bm, d), lambda i, j: (i, j, 0))],
        scratch_shapes=[
            pltpu.VMEM((nslab, CONV_HALO + bm, LANES), F32),
            pltpu.VMEM((nslab, bm, LANES), F32),
            pltpu.VMEM((bm, d), BF16),
        ],
        name="conv_module",
    )
    return out, casted


def _mlstm_proj_body(x_ref, g_ref, wq_ref, wk_ref, wv_ref, wo_ref, wi_ref, wf_ref, bo_ref, bi_ref, bf_ref,
                     q_ref, kT_ref, v_ref, og_ref, gi_ref, gf_ref):
    h = _rms_norm_rows(x_ref[...], g_ref[...]).astype(BF16)
    q = jnp.dot(h, wq_ref[...], preferred_element_type=F32)
    q_ref[...] = (q * (MLSTM_QK_DIM ** -0.5)).astype(BF16)
    kT_ref[...] = jnp.dot(h, wk_ref[...], preferred_element_type=F32).astype(BF16).T
    v_ref[...] = jnp.dot(h, wv_ref[...], preferred_element_type=F32).astype(BF16)
    o_pre = jnp.dot(h, wo_ref[...], preferred_element_type=F32) + bo_ref[...]
    og_ref[...] = jax.nn.sigmoid(o_pre).astype(BF16)
    gi_ref[...] = jnp.dot(h, wi_ref[...], preferred_element_type=F32) + bi_ref[...]
    gf_ref[...] = jnp.dot(h, wf_ref[...], preferred_element_type=F32) + bf_ref[...]


def _mlstm_proj(x3d, gain, w_in, wi, wf, bo, bi, bf, cast=(), *, bm=256):
    b, s, d = x3d.shape
    nqk = MLSTM_HEADS * MLSTM_QK_DIM
    tok = lambda n: pl.BlockSpec((None, bm, n), lambda i, j: (i, j, 0))
    cols = lambda n, c: pl.BlockSpec((d, n), lambda i, j: (0, c), pipeline_mode=pl.Buffered(1))
    assert d == 2 * nqk
    return _call_with_casts(
        _mlstm_proj_body,
        (x3d, gain.reshape(1, d), w_in, w_in, w_in, w_in, wi, wf, bo, bi, bf),
        cast,
        out_shape=[
            jax.ShapeDtypeStruct((b, s, nqk), BF16),
            jax.ShapeDtypeStruct((b, nqk, s), BF16),
            jax.ShapeDtypeStruct((b, s, d), BF16),
            jax.ShapeDtypeStruct((b, s, d), BF16),
            jax.ShapeDtypeStruct((b, s, LANES), F32),
            jax.ShapeDtypeStruct((b, s, LANES), F32),
        ],
        grid=(b, s // bm),
        in_specs=[
            tok(d),
            _resident((1, d)),
            cols(nqk, 0),
            cols(nqk, 1),
            cols(d, 1),
            cols(d, 2),
            _resident((d, LANES)),
            _resident((d, LANES)),
            _resident((1, d)),
            _resident((1, LANES)),
            _resident((1, LANES)),
        ],
        out_specs=[
            tok(nqk),
            pl.BlockSpec((None, nqk, bm), lambda i, j: (i, 0, j)),
            tok(d),
            tok(d),
            tok(LANES),
            tok(LANES),
        ],
        name="mlstm_proj",
    )


def _scan_rows(x, op, fill, period):
    r = lax.broadcasted_iota(jnp.int32, x.shape, 0) % period
    k = 1
    while k < period:
        x = op(x, jnp.where(r >= k, pltpu.roll(x, k, axis=0), fill))
        k *= 2
    return x


def _mlstm_gates_body(gi_ref, gf_ref, negm_ref, winter_ref, floor_ref, u_ref, kw_ref, dec_ref):
    s = gi_ref.shape[0]
    L = MLSTM_CHUNK
    nc = s // L
    gf = gf_ref[...]
    logf = jnp.minimum(gf, 0.0) - jnp.log(1.0 + jnp.exp(-jnp.abs(gf)))
    b = _scan_rows(logf, jnp.add, 0.0, L)
    u = gi_ref[...] - b
    cm = _scan_rows(u, jnp.maximum, -jnp.inf, L)
    m = jnp.zeros((1, LANES), F32)
    for c in range(nc):
        rows = slice(c * L, (c + 1) * L)
        cm_c = cm[rows]
        b_c = b[rows]
        u_c = u[rows]
        big_m = jnp.maximum(m, cm_c)
        m_last = big_m[L - 1:L, :]
        negm = -big_m
        wint = jnp.exp(m - big_m)
        flo = jnp.exp(-(b_c + big_m))
        for hh in range(MLSTM_HEADS):
            negm_ref[hh, rows, :] = jnp.broadcast_to(negm[:, hh:hh + 1], (L, LANES))
            winter_ref[hh, rows, :] = jnp.broadcast_to(wint[:, hh:hh + 1], (L, LANES))
            floor_ref[hh, rows, :] = jnp.broadcast_to(flo[:, hh:hh + 1], (L, LANES))
        uT = u_c.T
        mlT = jnp.broadcast_to(m_last, (L, LANES)).T
        mpT = jnp.broadcast_to(m, (L, LANES)).T
        u_ref[:, rows] = uT[:MLSTM_HEADS]
        kw_ref[:, rows] = jnp.exp(uT - mlT)[:MLSTM_HEADS]
        dec_ref[:, rows] = jnp.exp(mpT - mlT)[:MLSTM_HEADS]
        m = b_c[L - 1:L, :] + m_last


def _mlstm_gates(gi, gf):
    b, s, _ = gi.shape
    hcol = jax.ShapeDtypeStruct((b, MLSTM_HEADS, s, LANES), F32)
    hrow = jax.ShapeDtypeStruct((b, MLSTM_HEADS, s), F32)
    tok = pl.BlockSpec((None, s, LANES), lambda i: (i, 0, 0))
    col = pl.BlockSpec((None, MLSTM_HEADS, s, LANES), lambda i: (i, 0, 0, 0))
    rowspec = pl.BlockSpec((None, MLSTM_HEADS, s), lambda i: (i, 0, 0))
    return pl.pallas_call(
        _mlstm_gates_body,
        out_shape=(hcol, hcol, hcol, hrow, hrow, hrow),
        grid=(b,),
        in_specs=[tok, tok],
        out_specs=(col, col, col, rowspec, rowspec, rowspec),
        compiler_params=_params("parallel"),
        name="mlstm_gates",
    )(gi, gf)


def _mlstm_core_body(q_ref, kT_ref, v_ref, og_ref, hn_ref, negm_ref, winter_ref, floor_ref,
                     u_ref, kw_ref, dec_ref, o_ref):
    L = MLSTM_CHUNK
    dk, dv = MLSTM_QK_DIM, MLSTM_V_DIM
    nc = q_ref.shape[0] // L
    tril = lax.broadcasted_iota(jnp.int32, (L, L), 0) >= lax.broadcasted_iota(jnp.int32, (L, L), 1)

    def state_free(c):
        rows = slice(c * L, (c + 1) * L)
        kT = kT_ref[:, rows]
        qk = jnp.dot(q_ref[rows, :], kT, preferred_element_type=F32)
        kwT = kT.astype(F32) * kw_ref[:, rows]
        inc = jnp.dot(kwT.astype(BF16), v_ref[rows, :], preferred_element_type=F32)
        return qk, inc, jnp.sum(kwT, axis=-1, keepdims=True)

    cmat = jnp.zeros((dk, dv), F32)
    nvec = jnp.zeros((dk, LANES), F32)
    pending = [state_free(c) for c in range(min(MLSTM_LOOKAHEAD, nc))]
    for c in range(nc):
        rows = slice(c * L, (c + 1) * L)
        qk, inc, ksum = pending.pop(0)
        if c + MLSTM_LOOKAHEAD < nc:
            pending.append(state_free(c + MLSTM_LOOKAHEAD))
        q = q_ref[rows, :]
        wint = winter_ref[rows, :]
        inter = jnp.dot(q, cmat.astype(BF16), preferred_element_type=F32)
        qn_inter = jnp.dot(q, nvec.astype(BF16), preferred_element_type=F32)
        dmat = jnp.where(tril, jnp.exp(u_ref[:, rows] + negm_ref[rows, :]), 0.0)
        p = qk * dmat
        num = jnp.dot(p.astype(BF16), v_ref[rows, :], preferred_element_type=F32)
        num = num + jnp.concatenate([wint, wint], axis=1) * inter
        qn = jnp.sum(p, axis=-1, keepdims=True) + wint * qn_inter
        den = jnp.maximum(jnp.abs(qn), floor_ref[rows, :])
        r = 1.0 / den
        scale = r * lax.rsqrt(jnp.mean(num * num, axis=-1, keepdims=True) * r * r + EPS)
        y = num * jnp.concatenate([scale, scale], axis=1) * hn_ref[...]
        o_ref[rows, :] = (og_ref[rows, :].astype(F32) * y).astype(BF16)

        dec = dec_ref[:, rows]
        cmat = jnp.concatenate([dec, dec], axis=1) * cmat + inc
        nvec = dec * nvec + ksum


def _mlstm_core(q, kT, v, og, h_norm, negm, winter, floor, u, kw, dec):
    b, s, _ = q.shape
    dk, dv = MLSTM_QK_DIM, MLSTM_V_DIM
    d = MLSTM_HEADS * dv
    col = pl.BlockSpec((None, None, s, LANES), lambda i, j: (i, j, 0, 0))
    rowspec = pl.BlockSpec((None, None, 1, s), lambda i, j: (i, j, 0, 0))
    per_head_row = lambda a: a.reshape(b, MLSTM_HEADS, 1, s)
    return pl.pallas_call(
        _mlstm_core_body,
        out_shape=jax.ShapeDtypeStruct((b, s, d), BF16),
        grid=(b, MLSTM_HEADS),
        in_specs=[
            pl.BlockSpec((None, s, dk), lambda i, j: (i, 0, j)),
            pl.BlockSpec((None, dk, s), lambda i, j: (i, j, 0)),
            pl.BlockSpec((None, s, dv), lambda i, j: (i, 0, j)),
            pl.BlockSpec((None, s, dv), lambda i, j: (i, 0, j)),
            pl.BlockSpec((1, dv), lambda i, j: (0, j)),
            col, col, col, rowspec, rowspec, rowspec,
        ],
        out_specs=pl.BlockSpec((None, s, dv), lambda i, j: (i, 0, j)),
        compiler_params=_params("parallel", "parallel"),
        name="mlstm_core",
    )(q, kT, v, og, h_norm.reshape(1, d), negm, winter, floor, per_head_row(u), per_head_row(kw), per_head_row(dec))


def _mlstm_layer(x3d, mix_norm, w_in, b_gates, h_norm, w_out, cast=()):
    b, s, d = x3d.shape
    nh = MLSTM_HEADS
    nqk = nh * MLSTM_QK_DIM
    w_in_bf16 = w_in.astype(BF16)
    wg = w_in[:, 2 * nqk + 2 * d:]
    pad = lambda w: jnp.pad(w, ((0, 0), (0, LANES - nh))).astype(BF16)
    padb = lambda v: jnp.pad(v.astype(F32), (0, LANES - nh)).reshape(1, LANES)
    bo = b_gates[:d].astype(F32).reshape(1, d)
    (q, kT, v, og, gi, gf), (w_out_bf16, *casted) = _mlstm_proj(
        x3d, mix_norm, w_in_bf16, pad(wg[:, :nh]), pad(wg[:, nh:]),
        bo, padb(b_gates[d:d + nh]), padb(b_gates[d + nh:]), (w_out,) + tuple(cast))
    negm, winter, floor, u, kw, dec = _mlstm_gates(gi, gf)
    hgated = _mlstm_core(q, kT, v, og, h_norm.astype(F32), negm, winter, floor, u, kw, dec)
    out = _proj_res(hgated.reshape(b * s, d), w_out_bf16, jnp.zeros((d,), F32), x3d.reshape(b * s, d))
    return out.reshape(b, s, d), casted


def kernel(x, l0_mix_norm, l0_attn_w_qkv, l0_attn_q_norm, l0_attn_k_norm, l0_attn_sinks, l0_attn_w_o, l0_mlp_norm, l0_mlp_w1, l0_mlp_w2, l1_mix_norm, l1_conv_w_in, l1_conv_b_in, l1_conv_dw, l1_conv_dw_b, l1_conv_ln_g, l1_conv_ln_b, l1_conv_w_out, l1_conv_b_out, l1_mlp_norm, l1_mlp_w1, l1_mlp_w2, l2_mix_norm, l2_mlstm_w_in, l2_mlstm_b_gates, l2_mlstm_h_norm, l2_mlstm_w_out, l2_mlp_norm, l2_mlp_w1, l2_mlp_w2, l3_mix_norm, l3_attn_w_qkv, l3_attn_q_norm, l3_attn_k_norm, l3_attn_sinks, l3_attn_w_o, l3_mlp_norm, l3_mlp_w1, l3_mlp_w2):
    b, s, d = x.shape

    def mlp(x3d, norm, w1, w2):
        return _mlp(x3d.reshape(b * s, d), norm.astype(F32), w1, w2).reshape(b, s, d)

    x, (w1, w2, conv_w_in, conv_w_out) = _attention_layer(
        x, l0_mix_norm, l0_attn_w_qkv, l0_attn_q_norm, l0_attn_k_norm, l0_attn_sinks, l0_attn_w_o,
        cast=(l0_mlp_w1, l0_mlp_w2, l1_conv_w_in, l1_conv_w_out))
    x = mlp(x, l0_mlp_norm, w1, w2)
    x, (w1, w2) = _conv_layer(
        x, l1_mix_norm, conv_w_in, l1_conv_b_in, l1_conv_dw, l1_conv_dw_b, l1_conv_ln_g, l1_conv_ln_b, conv_w_out,
        l1_conv_b_out, cast=(l1_mlp_w1, l1_mlp_w2))
    x = mlp(x, l1_mlp_norm, w1, w2)
    x, (w1, w2) = _mlstm_layer(x, l2_mix_norm, l2_mlstm_w_in, l2_mlstm_b_gates, l2_mlstm_h_norm, l2_mlstm_w_out,
                               cast=(l2_mlp_w1, l2_mlp_w2))
    x = mlp(x, l2_mlp_norm, w1, w2)
    x, (w1, w2) = _attention_layer(
        x, l3_mix_norm, l3_attn_w_qkv, l3_attn_q_norm, l3_attn_k_norm, l3_attn_sinks, l3_attn_w_o,
        cast=(l3_mlp_w1, l3_mlp_w2))
    x = mlp(x, l3_mlp_norm, w1, w2)
    return x
```

```python
import functools

import jax
import jax.numpy as jnp
from jax import lax
from jax.experimental import pallas as pl
from jax.experimental.pallas import tpu as pltpu

F32 = jnp.float32
BF16 = jnp.bfloat16

EPS = 1e-6
NEG = -1e30
LOG2E = 1.4426950408889634

ATTN_HEADS = 32
ATTN_KV_HEADS = 4
ATTN_GROUP = ATTN_HEADS // ATTN_KV_HEADS
ATTN_HEAD_DIM = 64
ATTN_BLOCK = 128
CHUNK = 64
CONV_WIDTH = 31
MLSTM_HEADS = 8
MLSTM_QK_DIM = 128
MLSTM_V_DIM = 256
MLSTM_CHUNK = 128

LANES = 128
VMEM_LIMIT_BYTES = 60000 * 1024

QKV_ROW_SPLIT = 2
ATTN_STEP_BLOCKS = 4
ATTN_SCORE_LOOKAHEAD = 2
MLSTM_LOOKAHEAD = 1
CONV_HALO = 32


def _params(*sem):
    return pltpu.CompilerParams(dimension_semantics=sem, vmem_limit_bytes=VMEM_LIMIT_BYTES)


def _resident(shape):
    nd = len(shape)
    return pl.BlockSpec(shape, lambda *_: (0,) * nd, pipeline_mode=pl.Buffered(1))


def _rms_norm_rows(x, gain):
    return x * lax.rsqrt(jnp.mean(x * x, axis=-1, keepdims=True) + EPS) * gain


BF16_SUBLANES = 16


def _cast_specs(w, grid):
    rows, cols = w.shape
    steps = 1
    for g in grid:
        steps *= g
    per_step = rows // steps
    assert per_step * steps == rows and per_step >= 1
    blk = max(per_step, BF16_SUBLANES)
    share = blk // per_step

    def index_map(*g):
        flat = g[0]
        for n, i in zip(grid[1:], g[1:]):
            flat = flat * n + i
        return flat // share, 0

    return pl.BlockSpec((blk, cols), index_map)


def _call_with_casts(body, args, cast, *, grid, in_specs, out_specs, out_shape, scratch_shapes=(), name):
    n_in, n_out, n_cast = len(in_specs), len(out_specs), len(cast)

    def wrapped(*refs):
        cast_in = refs[n_in:n_in + n_cast]
        cast_out = refs[n_in + n_cast + n_out:n_in + 2 * n_cast + n_out]
        for src, dst in zip(cast_in, cast_out):
            dst[...] = src[...].astype(BF16)
        body(*refs[:n_in], *refs[n_in + n_cast:n_in + n_cast + n_out], *refs[n_in + 2 * n_cast + n_out:])

    side = [_cast_specs(w, grid) for w in cast]
    res = pl.pallas_call(
        wrapped,
        out_shape=list(out_shape) + [jax.ShapeDtypeStruct(w.shape, BF16) for w in cast],
        grid=grid,
        in_specs=list(in_specs) + side,
        out_specs=list(out_specs) + side,
        scratch_shapes=list(scratch_shapes),
        compiler_params=_params(*(("arbitrary",) * len(grid))),
        name=name,
    )(*args, *cast)
    return res[:n_out], res[n_out:]


def _mlp_body(x_ref, g_ref, w1_ref, w2_ref, o_ref, h_ref):
    def up_down(h):
        a = jnp.dot(h, w1_ref[...], preferred_element_type=F32)
        a = jnp.square(jnp.maximum(a, 0.0)).astype(BF16)
        return jnp.dot(a, w2_ref[...], preferred_element_type=F32)

    @pl.when(pl.program_id(1) == 0)
    def _():
        x = x_ref[...]
        h = _rms_norm_rows(x, g_ref[...]).astype(BF16)
        h_ref[...] = h
        o_ref[...] = x + up_down(h)

    @pl.when(pl.program_id(1) != 0)
    def _():
        o_ref[...] += up_down(h_ref[...])


def _mlp(x2d, gain, w1, w2, *, bm=1024, fc=1024):
    m, d = x2d.shape
    ff = w1.shape[1]
    return pl.pallas_call(
        _mlp_body,
        out_shape=jax.ShapeDtypeStruct((m, d), F32),
        grid=(m // bm, ff // fc),
        in_specs=[
            pl.BlockSpec((bm, d), lambda i, j: (i, 0)),
            pl.BlockSpec((1, d), lambda i, j: (0, 0)),
            pl.BlockSpec((d, fc), lambda i, j: (0, j)),
            pl.BlockSpec((fc, d), lambda i, j: (j, 0)),
        ],
        out_specs=pl.BlockSpec((bm, d), lambda i, j: (i, 0)),
        scratch_shapes=[pltpu.VMEM((bm, d), BF16)],
        compiler_params=_params("parallel", "arbitrary"),
        name="mlp",
    )(x2d, gain.reshape(1, d), w1, w2)


def _proj_res_body(a_ref, w_ref, b_ref, x_ref, o_ref):
    acc = jnp.dot(a_ref[...], w_ref[...], preferred_element_type=F32)
    o_ref[...] = x_ref[...] + acc + b_ref[...]


def _proj_res(a2d, w, bias, x2d, *, bm=512):
    m, k = a2d.shape
    d = w.shape[1]
    return pl.pallas_call(
        _proj_res_body,
        out_shape=jax.ShapeDtypeStruct((m, d), F32),
        grid=(m // bm,),
        in_specs=[
            pl.BlockSpec((bm, k), lambda i: (i, 0)),
            _resident((k, d)),
            _resident((1, d)),
            pl.BlockSpec((bm, d), lambda i: (i, 0)),
        ],
        out_specs=pl.BlockSpec((bm, d), lambda i: (i, 0)),
        compiler_params=_params("parallel"),
        name="proj_res",
    )(a2d, w, bias.reshape(1, d), x2d)


def _attn_qkv_body(x_ref, g_ref, wqvT_ref, wk_ref, kgain_ref, qT_ref, k_ref, vT_ref):
    nq = ATTN_HEADS * ATTN_HEAD_DIM
    bm = x_ref.shape[0]
    rt = bm // QKV_ROW_SPLIT
    prods = []
    for r in range(QKV_ROW_SPLIT):
        h = _rms_norm_rows(x_ref[r * rt:(r + 1) * rt, :], g_ref[...]).astype(BF16)
        qvT = lax.dot_general(wqvT_ref[...], h, (((1,), (1,)), ((), ())),
                              preferred_element_type=F32)
        kf = jnp.dot(h, wk_ref[...], preferred_element_type=F32)
        prods.append((qvT, kf))
    for r, (qvT, kf) in enumerate(prods):
        q3 = qvT[:nq].reshape(ATTN_HEADS, ATTN_HEAD_DIM, rt)
        inv = lax.rsqrt(jnp.mean(q3 * q3, axis=1, keepdims=True) + EPS) * (ATTN_HEAD_DIM ** -0.5)
        qT_ref[:, r * rt:(r + 1) * rt] = (q3 * inv).reshape(nq, rt).astype(BF16)
        vT_ref[:, r * rt:(r + 1) * rt] = qvT[nq:].astype(BF16)
        for j in range(ATTN_KV_HEADS):
            kj = kf[:, j * LANES:j * LANES + ATTN_HEAD_DIM]
            kn = kj * lax.rsqrt(jnp.mean(kj * kj, axis=-1, keepdims=True) + EPS)
            k_ref[j, r * rt:(r + 1) * rt, :] = (kn * kgain_ref[...]).astype(BF16)


def _attn_qkv(x3d, gain, wqvT, wk_pad, kgain, cast=(), *, bm=512):
    b, s, d = x3d.shape
    nq = ATTN_HEADS * ATTN_HEAD_DIM
    nkv = ATTN_KV_HEADS * ATTN_HEAD_DIM
    return _call_with_casts(
        _attn_qkv_body,
        (x3d, gain.reshape(1, d), wqvT, wk_pad, kgain.reshape(1, ATTN_HEAD_DIM)),
        cast,
        out_shape=[
            jax.ShapeDtypeStruct((b, nq, s), BF16),
            jax.ShapeDtypeStruct((b, ATTN_KV_HEADS, s, ATTN_HEAD_DIM), BF16),
            jax.ShapeDtypeStruct((b, nkv, s), BF16),
        ],
        grid=(b, s // bm),
        in_specs=[
            pl.BlockSpec((None, bm, d), lambda i, j: (i, j, 0)),
            _resident((1, d)),
            _resident((nq + nkv, d)),
            _resident((d, ATTN_KV_HEADS * LANES)),
            _resident((1, ATTN_HEAD_DIM)),
        ],
        out_specs=[
            pl.BlockSpec((None, nq, bm), lambda i, j: (i, 0, j)),
            pl.BlockSpec((None, ATTN_KV_HEADS, bm, ATTN_HEAD_DIM), lambda i, j: (i, 0, j, 0)),
            pl.BlockSpec((None, nkv, bm), lambda i, j: (i, 0, j)),
        ],
        name="attn_qkv",
    )


def _attn_core_body(qT_ref, kp_ref, ko_ref, vTp_ref, vTo_ref, sink_ref, wo_ref, x_ref, out_ref, o_scr,
                    *, steps_per_seq, n_steps):
    t = pl.program_id(0)
    first_of_seq = jnp.minimum(t, n_steps - 1) % steps_per_seq == 0

    @pl.when(t == 0)
    def _():
        o_scr[...] = jnp.zeros(o_scr.shape, BF16)

    qb, hd, ck, grp = ATTN_BLOCK, ATTN_HEAD_DIM, CHUNK, ATTN_GROUP
    nvis = 3 * ck
    low = lax.broadcasted_iota(jnp.int32, (hd, LANES), 1) < ck
    no_prev = jnp.where(first_of_seq, NEG, 0.0)
    zeros = jnp.zeros((ck, grp * ck), BF16)
    nchunk = qb // ck

    def band(a, h):
        rows = slice(h * hd, (h + 1) * hd)
        own = slice(a * qb, (a + 1) * qb)
        if a == 0:
            k_prev, v_prev = kp_ref[h], vTp_ref[rows, :]
        else:
            k_prev, v_prev = ko_ref[h, (a - 1) * qb:a * qb, :], vTo_ref[rows, (a - 1) * qb:a * qb]
        return (jnp.concatenate([k_prev, ko_ref[h, own, :]], axis=0),
                jnp.concatenate([v_prev, vTo_ref[rows, own]], axis=1))

    def scores(a, h, c):
        tiles = [qT_ref[(h * grp + g) * hd:(h * grp + g + 1) * hd, a * qb:(a + 1) * qb] for g in range(grp)]
        if c == 0:
            pairs = [jnp.where(low, tiles[2 * j], pltpu.roll(tiles[2 * j + 1], ck, axis=1)) for j in range(grp // 2)]
        else:
            pairs = [jnp.where(low, pltpu.roll(tiles[2 * j], ck, axis=1), tiles[2 * j + 1]) for j in range(grp // 2)]
        q = jnp.concatenate(pairs, axis=1)
        kb, _ = band(a, h)
        return jnp.dot(kb[c * ck:c * ck + nvis, :], q, preferred_element_type=F32)

    def attend(a, h, c, s):
        if a == 0:
            nprev = qb - c * ck
            s = jnp.concatenate([s[:nprev] + no_prev, s[nprev:]], axis=0)
        sk = sink_ref[h]
        m = jnp.maximum(jnp.max(s, axis=0, keepdims=True), sk)
        p = jnp.exp2(s - m)
        l = jnp.sum(p, axis=0, keepdims=True) + jnp.exp2(sk - m)
        pb = p.astype(BF16)
        pb = jnp.concatenate([pb, zeros] if c == 0 else [zeros, pb], axis=0)
        _, vb = band(a, h)
        return (jnp.dot(vb, pb, preferred_element_type=F32) / l).astype(BF16)

    subs = [(a, h, c) for a in range(ATTN_STEP_BLOCKS) for h in range(ATTN_KV_HEADS) for c in range(nchunk)]
    d_model = wo_ref.shape[1]
    n_proj = len(subs) // ATTN_STEP_BLOCKS
    pw = d_model // n_proj

    def project(i):
        cols = slice(i * pw, (i + 1) * pw)
        out_ref[:, cols] = x_ref[:, cols] + jnp.dot(o_scr[...], wo_ref[:, cols], preferred_element_type=F32)

    ahead = ATTN_SCORE_LOOKAHEAD
    pending = [scores(*sub) for sub in subs[:ahead]]
    outs = {}
    new_tiles = []
    for i, (a, h, c) in enumerate(subs):
        s_cur = pending.pop(0)
        if i + ahead < len(subs):
            pending.append(scores(*subs[i + ahead]))
        outs[c] = attend(a, h, c, s_cur)
        if i % ATTN_STEP_BLOCKS == ATTN_STEP_BLOCKS - 1:
            project(i // ATTN_STEP_BLOCKS)
        if c == nchunk - 1:
            for j in range(grp // 2):
                a0 = outs[0][:, j * LANES:(j + 1) * LANES]
                a1 = outs[1][:, j * LANES:(j + 1) * LANES]
                even = jnp.where(low, a0, pltpu.roll(a1, ck, axis=1))
                odd = jnp.where(low, pltpu.roll(a0, ck, axis=1), a1)
                new_tiles.append((a, h * (grp // 2) + j, jnp.concatenate([even, odd], axis=0).T))
    for a, p_idx, tile in new_tiles:
        o_scr[a * qb:(a + 1) * qb, p_idx * LANES:(p_idx + 1) * LANES] = tile


def _attn_core(qT, k, vT, sink_rows, w_o, x2d):
    b, nq, s = qT.shape
    nkv = vT.shape[1]
    d = w_o.shape[1]
    qb = ATTN_BLOCK
    sb = ATTN_STEP_BLOCKS * qb
    per_seq = s // sb
    total = b * per_seq
    cur = lambda t: jnp.minimum(t, total - 1)
    seq = lambda t: cur(t) // per_seq
    grp_i = lambda t: cur(t) % per_seq
    prev = lambda t: jnp.maximum(grp_i(t) * ATTN_STEP_BLOCKS - 1, 0)
    done = lambda t: jnp.maximum(t - 1, 0)
    return pl.pallas_call(
        functools.partial(_attn_core_body, steps_per_seq=per_seq, n_steps=total),
        out_shape=jax.ShapeDtypeStruct((b * s, d), F32),
        grid=(total + 1,),
        in_specs=[
            pl.BlockSpec((None, nq, sb), lambda t: (seq(t), 0, grp_i(t))),
            pl.BlockSpec((None, ATTN_KV_HEADS, qb, ATTN_HEAD_DIM), lambda t: (seq(t), 0, prev(t), 0)),
            pl.BlockSpec((None, ATTN_KV_HEADS, sb, ATTN_HEAD_DIM), lambda t: (seq(t), 0, grp_i(t), 0)),
            pl.BlockSpec((None, nkv, qb), lambda t: (seq(t), 0, prev(t))),
            pl.BlockSpec((None, nkv, sb), lambda t: (seq(t), 0, grp_i(t))),
            _resident((ATTN_KV_HEADS, 1, ATTN_GROUP * CHUNK)),
            _resident((nq, d)),
            pl.BlockSpec((sb, d), lambda t: (done(t), 0)),
        ],
        out_specs=pl.BlockSpec((sb, d), lambda t: (done(t), 0)),
        scratch_shapes=[pltpu.VMEM((sb, nq), BF16)],
        compiler_params=_params("arbitrary"),
        name="attn_core",
    )(qT, k, k, vT, vT, sink_rows, w_o, x2d)


def _attention_layer(x3d, mix_norm, w_qkv, q_norm, k_norm, sinks, w_o, cast=()):
    b, s, d = x3d.shape
    nq = ATTN_HEADS * ATTN_HEAD_DIM
    nkv = ATTN_KV_HEADS * ATTN_HEAD_DIM
    wq, wk, wv = w_qkv[:, :nq], w_qkv[:, nq:nq + nkv], w_qkv[:, nq + nkv:]
    wqvT = jnp.concatenate([wq, wv], axis=1).T.astype(BF16)
    wk_pad = jnp.pad(wk.reshape(d, ATTN_KV_HEADS, ATTN_HEAD_DIM),
                     ((0, 0), (0, 0), (0, LANES - ATTN_HEAD_DIM))).reshape(d, ATTN_KV_HEADS * LANES).astype(BF16)
    kgain = (q_norm * k_norm).astype(F32) * LOG2E
    sink_rows = jnp.repeat(sinks.astype(F32).reshape(ATTN_KV_HEADS, 1, ATTN_GROUP) * LOG2E, CHUNK, axis=2)
    (qT, k, vT), (w_o_bf16, *casted) = _attn_qkv(x3d, mix_norm, wqvT, wk_pad, kgain, (w_o,) + tuple(cast))
    out = _attn_core(qT, k, vT, sink_rows, w_o_bf16, x3d.reshape(b * s, d))
    return out.reshape(b, s, d), casted


def _conv_body(x_ref, g_ref, win_ref, bin_ref, dw_ref, dwb_ref, lng_ref, lnb_ref, wout_ref, bout_ref,
               o_ref, gbuf, cbuf, ybuf):
    bm, d = x_ref.shape
    nslab = d // LANES

    @pl.when(pl.program_id(1) == 0)
    def _():
        gbuf[:, 0:CONV_HALO, :] = jnp.zeros((nslab, CONV_HALO, LANES), F32)

    x = x_ref[...]
    h = _rms_norm_rows(x, g_ref[...]).astype(BF16)
    u = jnp.dot(h, win_ref[...], preferred_element_type=F32) + bin_ref[...]
    glu = u[:, :d] * jax.nn.sigmoid(u[:, d:])
    for j in range(nslab):
        gbuf[j, CONV_HALO:CONV_HALO + bm, :] = glu[:, j * LANES:(j + 1) * LANES]

    def slab(j, carry):
        acc = jnp.broadcast_to(dwb_ref[j], (bm, LANES))
        taps = dw_ref[j]
        for t in range(CONV_WIDTH):
            r0 = CONV_HALO - (CONV_WIDTH - 1) + t
            acc = acc + taps[t:t + 1, :] * gbuf[j, pl.ds(r0, bm), :]
        cbuf[j] = acc
        gbuf[j, 0:CONV_HALO, :] = gbuf[j, pl.ds(bm, CONV_HALO), :]
        return carry

    lax.fori_loop(0, nslab, slab, 0)

    tot = cbuf[0]
    for j in range(1, nslab):
        tot = tot + cbuf[j]
    mu = jnp.sum(tot, axis=-1, keepdims=True) * (1.0 / d)
    sq = None
    for j in range(nslab):
        c = cbuf[j] - mu
        sq = c * c if sq is None else sq + c * c
    inv = lax.rsqrt(jnp.sum(sq, axis=-1, keepdims=True) * (1.0 / d) + EPS)
    for j in range(nslab):
        y = (cbuf[j] - mu) * inv * lng_ref[:, j * LANES:(j + 1) * LANES] + lnb_ref[:, j * LANES:(j + 1) * LANES]
        ybuf[:, j * LANES:(j + 1) * LANES] = (y * jax.nn.sigmoid(y)).astype(BF16)
    o_ref[...] = x + jnp.dot(ybuf[...], wout_ref[...], preferred_element_type=F32) + bout_ref[...]


def _conv_layer(x3d, mix_norm, w_in, b_in, dw, dw_b, ln_g, ln_b, w_out, b_out, cast=(), *, bm=256):
    b, s, d = x3d.shape
    nslab = d // LANES
    dw_slabs = dw.astype(F32).reshape(CONV_WIDTH, nslab, LANES).transpose(1, 0, 2)
    dwb_slabs = dw_b.astype(F32).reshape(nslab, 1, LANES)
    row = lambda v: v.astype(F32).reshape(1, -1)
    (out,), casted = _call_with_casts(
        _conv_body,
        (x3d, row(mix_norm), w_in.astype(BF16), row(b_in), dw_slabs, dwb_slabs, row(ln_g), row(ln_b),
         w_out.astype(BF16), row(b_out)),
        cast,
        out_shape=[jax.ShapeDtypeStruct((b, s, d), F32)],
        grid=(b, s // bm),
        in_specs=[
            pl.BlockSpec((None, bm, d), lambda i, j: (i, j, 0)),
            _resident((1, d)),
            _resident((d, 2 * d)),
            _resident((1, 2 * d)),
            _resident((nslab, CONV_WIDTH, LANES)),
            _resident((nslab, 1, LANES)),
            _resident((1, d)),
            _resident((1, d)),
            _resident((d, d)),
            _resident((1, d)),
        ],
        out_specs=[pl.BlockSpec((None, bm, d), lambda i, j: (i, j, 0))],
        scratch_shapes=[
            pltpu.VMEM((nslab, CONV_HALO + bm, LANES), F32),
            pltpu.VMEM((nslab, bm, LANES), F32),
            pltpu.VMEM((bm, d), BF16),
        ],
        name="conv_module",
    )
    return out, casted


def _mlstm_proj_body(x_ref, g_ref, wq_ref, wk_ref, wv_ref, wo_ref, wi_ref, wf_ref, bo_ref, bi_ref, bf_ref,
                     q_ref, kT_ref, v_ref, og_ref, gi_ref, gf_ref):
    h = _rms_norm_rows(x_ref[...], g_ref[...]).astype(BF16)
    q = jnp.dot(h, wq_ref[...], preferred_element_type=F32)
    q_ref[...] = (q * (MLSTM_QK_DIM ** -0.5)).astype(BF16)
    kT_ref[...] = jnp.dot(h, wk_ref[...], preferred_element_type=F32).astype(BF16).T
    v_ref[...] = jnp.dot(h, wv_ref[...], preferred_element_type=F32).astype(BF16)
    o_pre = jnp.dot(h, wo_ref[...], preferred_element_type=F32) + bo_ref[...]
    og_ref[...] = jax.nn.sigmoid(o_pre).astype(BF16)
    gi_ref[...] = jnp.dot(h, wi_ref[...], preferred_element_type=F32) + bi_ref[...]
    gf_ref[...] = jnp.dot(h, wf_ref[...], preferred_element_type=F32) + bf_ref[...]


def _mlstm_proj(x3d, gain, w_in, wi, wf, bo, bi, bf, cast=(), *, bm=256):
    b, s, d = x3d.shape
    nqk = MLSTM_HEADS * MLSTM_QK_DIM
    tok = lambda n: pl.BlockSpec((None, bm, n), lambda i, j: (i, j, 0))
    cols = lambda n, c: pl.BlockSpec((d, n), lambda i, j: (0, c), pipeline_mode=pl.Buffered(1))
    assert d == 2 * nqk
    return _call_with_casts(
        _mlstm_proj_body,
        (x3d, gain.reshape(1, d), w_in, w_in, w_in, w_in, wi, wf, bo, bi, bf),
        cast,
        out_shape=[
            jax.ShapeDtypeStruct((b, s, nqk), BF16),
            jax.ShapeDtypeStruct((b, nqk, s), BF16),
            jax.ShapeDtypeStruct((b, s, d), BF16),
            jax.ShapeDtypeStruct((b, s, d), BF16),
            jax.ShapeDtypeStruct((b, s, LANES), F32),
            jax.ShapeDtypeStruct((b, s, LANES), F32),
        ],
        grid=(b, s // bm),
        in_specs=[
            tok(d),
            _resident((1, d)),
            cols(nqk, 0),
            cols(nqk, 1),
            cols(d, 1),
            cols(d, 2),
            _resident((d, LANES)),
            _resident((d, LANES)),
            _resident((1, d)),
            _resident((1, LANES)),
            _resident((1, LANES)),
        ],
        out_specs=[
            tok(nqk),
            pl.BlockSpec((None, nqk, bm), lambda i, j: (i, 0, j)),
            tok(d),
            tok(d),
            tok(LANES),
            tok(LANES),
        ],
        name="mlstm_proj",
    )


def _scan_rows(x, op, fill, period):
    r = lax.broadcasted_iota(jnp.int32, x.shape, 0) % period
    k = 1
    while k < period:
        x = op(x, jnp.where(r >= k, pltpu.roll(x, k, axis=0), fill))
        k *= 2
    return x


def _mlstm_gates_body(gi_ref, gf_ref, negm_ref, winter_ref, floor_ref, u_ref, kw_ref, dec_ref):
    s = gi_ref.shape[0]
    L = MLSTM_CHUNK
    nc = s // L
    gf = gf_ref[...]
    logf = jnp.minimum(gf, 0.0) - jnp.log(1.0 + jnp.exp(-jnp.abs(gf)))
    b = _scan_rows(logf, jnp.add, 0.0, L)
    u = gi_ref[...] - b
    cm = _scan_rows(u, jnp.maximum, -jnp.inf, L)
    m = jnp.zeros((1, LANES), F32)
    for c in range(nc):
        rows = slice(c * L, (c + 1) * L)
        cm_c = cm[rows]
        b_c = b[rows]
        u_c = u[rows]
        big_m = jnp.maximum(m, cm_c)
        m_last = big_m[L - 1:L, :]
        negm_ref[:, rows] = (-big_m).T[:MLSTM_HEADS]
        winter_ref[:, rows] = jnp.exp(m - big_m).T[:MLSTM_HEADS]
        floor_ref[:, rows] = jnp.exp(-(b_c + big_m)).T[:MLSTM_HEADS]
        uT = u_c.T
        mlT = jnp.broadcast_to(m_last, (L, LANES)).T
        mpT = jnp.broadcast_to(m, (L, LANES)).T
        u_ref[:, rows] = uT[:MLSTM_HEADS]
        kw_ref[:, rows] = jnp.exp(uT - mlT)[:MLSTM_HEADS]
        dec_ref[:, rows] = jnp.exp(mpT - mlT)[:MLSTM_HEADS]
        m = b_c[L - 1:L, :] + m_last


def _mlstm_gates(gi, gf):
    b, s, _ = gi.shape
    hrow = jax.ShapeDtypeStruct((b, MLSTM_HEADS, s), F32)
    tok = pl.BlockSpec((None, s, LANES), lambda i: (i, 0, 0))
    rowspec = pl.BlockSpec((None, MLSTM_HEADS, s), lambda i: (i, 0, 0))
    return pl.pallas_call(
        _mlstm_gates_body,
        out_shape=(hrow,) * 6,
        grid=(b,),
        in_specs=[tok, tok],
        out_specs=(rowspec,) * 6,
        compiler_params=_params("parallel"),
        name="mlstm_gates",
    )(gi, gf)


def _mlstm_core_body(q_ref, kT_ref, v_ref, og_ref, hn_ref, negm_ref, winter_ref, floor_ref,
                     u_ref, kw_ref, dec_ref, o_ref):
    L = MLSTM_CHUNK
    dk, dv = MLSTM_QK_DIM, MLSTM_V_DIM
    nc = q_ref.shape[0] // L
    tril = lax.broadcasted_iota(jnp.int32, (L, L), 0) >= lax.broadcasted_iota(jnp.int32, (L, L), 1)

    def state_free(c):
        rows = slice(c * L, (c + 1) * L)
        kT = kT_ref[:, rows]
        qk = jnp.dot(q_ref[rows, :], kT, preferred_element_type=F32)
        kwT = kT.astype(F32) * kw_ref[:, rows]
        inc = jnp.dot(kwT.astype(BF16), v_ref[rows, :], preferred_element_type=F32)
        return qk, inc, jnp.sum(kwT, axis=-1, keepdims=True)

    cmat = jnp.zeros((dk, dv), F32)
    nvec = jnp.zeros((dk, LANES), F32)
    pending = [state_free(c) for c in range(min(MLSTM_LOOKAHEAD, nc))]
    for c in range(nc):
        rows = slice(c * L, (c + 1) * L)
        qk, inc, ksum = pending.pop(0)
        if c + MLSTM_LOOKAHEAD < nc:
            pending.append(state_free(c + MLSTM_LOOKAHEAD))
        q = q_ref[rows, :]
        column = lambda ref: jnp.broadcast_to(ref[:, rows], (LANES, L)).T
        wint = column(winter_ref)
        inter = jnp.dot(q, cmat.astype(BF16), preferred_element_type=F32)
        qn_inter = jnp.dot(q, nvec.astype(BF16), preferred_element_type=F32)
        dmat = jnp.where(tril, jnp.exp(u_ref[:, rows] + column(negm_ref)), 0.0)
        p = qk * dmat
        num = jnp.dot(p.astype(BF16), v_ref[rows, :], preferred_element_type=F32)
        num = num + jnp.concatenate([wint, wint], axis=1) * inter
        qn = jnp.sum(p, axis=-1, keepdims=True) + wint * qn_inter
        den = jnp.maximum(jnp.abs(qn), column(floor_ref))
        r = 1.0 / den
        scale = r * lax.rsqrt(jnp.mean(num * num, axis=-1, keepdims=True) * r * r + EPS)
        y = num * jnp.concatenate([scale, scale], axis=1) * hn_ref[...]
        o_ref[rows, :] = (og_ref[rows, :].astype(F32) * y).astype(BF16)

        dec = dec_ref[:, rows]
        cmat = jnp.concatenate([dec, dec], axis=1) * cmat + inc
        nvec = dec * nvec + ksum


def _mlstm_core(q, kT, v, og, h_norm, negm, winter, floor, u, kw, dec):
    b, s, _ = q.shape
    dk, dv = MLSTM_QK_DIM, MLSTM_V_DIM
    d = MLSTM_HEADS * dv
    rowspec = pl.BlockSpec((None, None, 1, s), lambda i, j: (i, j, 0, 0))
    per_head_row = lambda a: a.reshape(b, MLSTM_HEADS, 1, s)
    return pl.pallas_call(
        _mlstm_core_body,
        out_shape=jax.ShapeDtypeStruct((b, s, d), BF16),
        grid=(b, MLSTM_HEADS),
        in_specs=[
            pl.BlockSpec((None, s, dk), lambda i, j: (i, 0, j)),
            pl.BlockSpec((None, dk, s), lambda i, j: (i, j, 0)),
            pl.BlockSpec((None, s, dv), lambda i, j: (i, 0, j)),
            pl.BlockSpec((None, s, dv), lambda i, j: (i, 0, j)),
            pl.BlockSpec((1, dv), lambda i, j: (0, j)),
            rowspec, rowspec, rowspec, rowspec, rowspec, rowspec,
        ],
        out_specs=pl.BlockSpec((None, s, dv), lambda i, j: (i, 0, j)),
        compiler_params=_params("parallel", "parallel"),
        name="mlstm_core",
    )(q, kT, v, og, h_norm.reshape(1, d), *(per_head_row(a) for a in (negm, winter, floor, u, kw, dec)))


def _mlstm_layer(x3d, mix_norm, w_in, b_gates, h_norm, w_out, cast=()):
    b, s, d = x3d.shape
    nh = MLSTM_HEADS
    nqk = nh * MLSTM_QK_DIM
    w_in_bf16 = w_in.astype(BF16)
    wg = w_in[:, 2 * nqk + 2 * d:]
    pad = lambda w: jnp.pad(w, ((0, 0), (0, LANES - nh))).astype(BF16)
    padb = lambda v: jnp.pad(v.astype(F32), (0, LANES - nh)).reshape(1, LANES)
    bo = b_gates[:d].astype(F32).reshape(1, d)
    (q, kT, v, og, gi, gf), (w_out_bf16, *casted) = _mlstm_proj(
        x3d, mix_norm, w_in_bf16, pad(wg[:, :nh]), pad(wg[:, nh:]),
        bo, padb(b_gates[d:d + nh]), padb(b_gates[d + nh:]), (w_out,) + tuple(cast))
    negm, winter, floor, u, kw, dec = _mlstm_gates(gi, gf)
    hgated = _mlstm_core(q, kT, v, og, h_norm.astype(F32), negm, winter, floor, u, kw, dec)
    out = _proj_res(hgated.reshape(b * s, d), w_out_bf16, jnp.zeros((d,), F32), x3d.reshape(b * s, d))
    return out.reshape(b, s, d), casted


def kernel(x, l0_mix_norm, l0_attn_w_qkv, l0_attn_q_norm, l0_attn_k_norm, l0_attn_sinks, l0_attn_w_o, l0_mlp_norm, l0_mlp_w1, l0_mlp_w2, l1_mix_norm, l1_conv_w_in, l1_conv_b_in, l1_conv_dw, l1_conv_dw_b, l1_conv_ln_g, l1_conv_ln_b, l1_conv_w_out, l1_conv_b_out, l1_mlp_norm, l1_mlp_w1, l1_mlp_w2, l2_mix_norm, l2_mlstm_w_in, l2_mlstm_b_gates, l2_mlstm_h_norm, l2_mlstm_w_out, l2_mlp_norm, l2_mlp_w1, l2_mlp_w2, l3_mix_norm, l3_attn_w_qkv, l3_attn_q_norm, l3_attn_k_norm, l3_attn_sinks, l3_attn_w_o, l3_mlp_norm, l3_mlp_w1, l3_mlp_w2):
    b, s, d = x.shape

    def mlp(x3d, norm, w1, w2):
        return _mlp(x3d.reshape(b * s, d), norm.astype(F32), w1, w2).reshape(b, s, d)

    x, (w1, w2, conv_w_in, conv_w_out) = _attention_layer(
        x, l0_mix_norm, l0_attn_w_qkv, l0_attn_q_norm, l0_attn_k_norm, l0_attn_sinks, l0_attn_w_o,
        cast=(l0_mlp_w1, l0_mlp_w2, l1_conv_w_in, l1_conv_w_out))
    x = mlp(x, l0_mlp_norm, w1, w2)
    x, (w1, w2) = _conv_layer(
        x, l1_mix_norm, conv_w_in, l1_conv_b_in, l1_conv_dw, l1_conv_dw_b, l1_conv_ln_g, l1_conv_ln_b, conv_w_out,
        l1_conv_b_out, cast=(l1_mlp_w1, l1_mlp_w2))
    x = mlp(x, l1_mlp_norm, w1, w2)
    x, (w1, w2) = _mlstm_layer(x, l2_mix_norm, l2_mlstm_w_in, l2_mlstm_b_gates, l2_mlstm_h_norm, l2_mlstm_w_out,
                               cast=(l2_mlp_w1, l2_mlp_w2))
    x = mlp(x, l2_mlp_norm, w1, w2)
    x, (w1, w2) = _attention_layer(
        x, l3_mix_norm, l3_attn_w_qkv, l3_attn_q_norm, l3_attn_k_norm, l3_attn_sinks, l3_attn_w_o,
        cast=(l3_mlp_w1, l3_mlp_w2))
    x = mlp(x, l3_mlp_norm, w1, w2)
    return x
```

```python
import functools

import jax
import jax.numpy as jnp
from jax import lax
from jax.experimental import pallas as pl
from jax.experimental.pallas import tpu as pltpu

F32 = jnp.float32
BF16 = jnp.bfloat16

EPS = 1e-6
NEG = -1e30
LOG2E = 1.4426950408889634

ATTN_HEADS = 32
ATTN_KV_HEADS = 4
ATTN_GROUP = ATTN_HEADS // ATTN_KV_HEADS
ATTN_HEAD_DIM = 64
ATTN_BLOCK = 128
CHUNK = 64
CONV_WIDTH = 31
MLSTM_HEADS = 8
MLSTM_QK_DIM = 128
MLSTM_V_DIM = 256
MLSTM_CHUNK = 128

LANES = 128
VMEM_LIMIT_BYTES = 60000 * 1024

QKV_ROW_SPLIT = 2
ATTN_STEP_BLOCKS = 4
ATTN_SCORE_LOOKAHEAD = 2
MLSTM_LOOKAHEAD = 1
CONV_HALO = 32


def _params(*sem):
    return pltpu.CompilerParams(dimension_semantics=sem, vmem_limit_bytes=VMEM_LIMIT_BYTES)


def _resident(shape):
    nd = len(shape)
    return pl.BlockSpec(shape, lambda *_: (0,) * nd, pipeline_mode=pl.Buffered(1))


def _rms_norm_rows(x, gain):
    return x * lax.rsqrt(jnp.mean(x * x, axis=-1, keepdims=True) + EPS) * gain


BF16_SUBLANES = 16


def _cast_specs(w, grid):
    rows, cols = w.shape
    steps = 1
    for g in grid:
        steps *= g
    per_step = rows // steps
    assert per_step * steps == rows and per_step >= 1
    blk = max(per_step, BF16_SUBLANES)
    share = blk // per_step

    def index_map(*g):
        flat = g[0]
        for n, i in zip(grid[1:], g[1:]):
            flat = flat * n + i
        return flat // share, 0

    return pl.BlockSpec((blk, cols), index_map)


def _call_with_casts(body, args, cast, *, grid, in_specs, out_specs, out_shape, scratch_shapes=(), name):
    n_in, n_out, n_cast = len(in_specs), len(out_specs), len(cast)

    def wrapped(*refs):
        cast_in = refs[n_in:n_in + n_cast]
        cast_out = refs[n_in + n_cast + n_out:n_in + 2 * n_cast + n_out]
        for src, dst in zip(cast_in, cast_out):
            dst[...] = src[...].astype(BF16)
        body(*refs[:n_in], *refs[n_in + n_cast:n_in + n_cast + n_out], *refs[n_in + 2 * n_cast + n_out:])

    side = [_cast_specs(w, grid) for w in cast]
    res = pl.pallas_call(
        wrapped,
        out_shape=list(out_shape) + [jax.ShapeDtypeStruct(w.shape, BF16) for w in cast],
        grid=grid,
        in_specs=list(in_specs) + side,
        out_specs=list(out_specs) + side,
        scratch_shapes=list(scratch_shapes),
        compiler_params=_params(*(("arbitrary",) * len(grid))),
        name=name,
    )(*args, *cast)
    return res[:n_out], res[n_out:]


def _mlp_body(x_ref, g_ref, w1_ref, w2_ref, o_ref, h_ref):
    def up_down(h):
        a = jnp.dot(h, w1_ref[...], preferred_element_type=F32)
        a = jnp.square(jnp.maximum(a, 0.0)).astype(BF16)
        return jnp.dot(a, w2_ref[...], preferred_element_type=F32)

    @pl.when(pl.program_id(1) == 0)
    def _():
        x = x_ref[...]
        h = _rms_norm_rows(x, g_ref[...]).astype(BF16)
        h_ref[...] = h
        o_ref[...] = x + up_down(h)

    @pl.when(pl.program_id(1) != 0)
    def _():
        o_ref[...] += up_down(h_ref[...])


def _mlp(x2d, gain, w1, w2, *, bm=1024, fc=1024):
    m, d = x2d.shape
    ff = w1.shape[1]
    return pl.pallas_call(
        _mlp_body,
        out_shape=jax.ShapeDtypeStruct((m, d), F32),
        grid=(m // bm, ff // fc),
        in_specs=[
            pl.BlockSpec((bm, d), lambda i, j: (i, 0)),
            pl.BlockSpec((1, d), lambda i, j: (0, 0)),
            pl.BlockSpec((d, fc), lambda i, j: (0, j)),
            pl.BlockSpec((fc, d), lambda i, j: (j, 0)),
        ],
        out_specs=pl.BlockSpec((bm, d), lambda i, j: (i, 0)),
        scratch_shapes=[pltpu.VMEM((bm, d), BF16)],
        compiler_params=_params("parallel", "arbitrary"),
        name="mlp",
    )(x2d, gain.reshape(1, d), w1, w2)


def _proj_res_body(a_ref, w_ref, b_ref, x_ref, o_ref):
    acc = jnp.dot(a_ref[...], w_ref[...], preferred_element_type=F32)
    o_ref[...] = x_ref[...] + acc + b_ref[...]


def _proj_res(a2d, w, bias, x2d, *, bm=512):
    m, k = a2d.shape
    d = w.shape[1]
    return pl.pallas_call(
        _proj_res_body,
        out_shape=jax.ShapeDtypeStruct((m, d), F32),
        grid=(m // bm,),
        in_specs=[
            pl.BlockSpec((bm, k), lambda i: (i, 0)),
            _resident((k, d)),
            _resident((1, d)),
            pl.BlockSpec((bm, d), lambda i: (i, 0)),
        ],
        out_specs=pl.BlockSpec((bm, d), lambda i: (i, 0)),
        compiler_params=_params("parallel"),
        name="proj_res",
    )(a2d, w, bias.reshape(1, d), x2d)


def _attn_qkv_body(x_ref, g_ref, wqT_ref, wvT_ref, wk_ref, kgain_ref, qT_ref, k_ref, vT_ref):
    nq = ATTN_HEADS * ATTN_HEAD_DIM
    bm = x_ref.shape[0]
    rt = bm // QKV_ROW_SPLIT
    nt = (((1,), (1,)), ((), ()))
    prods = []
    wqvT = jnp.concatenate([wqT_ref[...], wvT_ref[...]], axis=0)
    for r in range(QKV_ROW_SPLIT):
        h = _rms_norm_rows(x_ref[r * rt:(r + 1) * rt, :], g_ref[...]).astype(BF16)
        qvT = lax.dot_general(wqvT, h, nt, preferred_element_type=F32)
        kf = jnp.dot(h, wk_ref[...], preferred_element_type=F32)
        prods.append((qvT[:nq], qvT[nq:], kf))
    for r, (qT, vT, kf) in enumerate(prods):
        q3 = qT.reshape(ATTN_HEADS, ATTN_HEAD_DIM, rt)
        inv = lax.rsqrt(jnp.mean(q3 * q3, axis=1, keepdims=True) + EPS) * (ATTN_HEAD_DIM ** -0.5)
        qT_ref[:, r * rt:(r + 1) * rt] = (q3 * inv).reshape(nq, rt).astype(BF16)
        vT_ref[:, r * rt:(r + 1) * rt] = vT.astype(BF16)
        for j in range(ATTN_KV_HEADS):
            kj = kf[:, j * LANES:j * LANES + ATTN_HEAD_DIM]
            kn = kj * lax.rsqrt(jnp.mean(kj * kj, axis=-1, keepdims=True) + EPS)
            k_ref[j, r * rt:(r + 1) * rt, :] = (kn * kgain_ref[...]).astype(BF16)


def _attn_qkv(x3d, gain, w_qkvT, wk_pad, kgain, cast=(), *, bm=512):
    b, s, d = x3d.shape
    nq = ATTN_HEADS * ATTN_HEAD_DIM
    nkv = ATTN_KV_HEADS * ATTN_HEAD_DIM
    assert nq % nkv == 0
    rows = lambda n, r: pl.BlockSpec((n, d), lambda i, j: (r, 0), pipeline_mode=pl.Buffered(1))
    return _call_with_casts(
        _attn_qkv_body,
        (x3d, gain.reshape(1, d), w_qkvT, w_qkvT, wk_pad, kgain.reshape(1, ATTN_HEAD_DIM)),
        cast,
        out_shape=[
            jax.ShapeDtypeStruct((b, nq, s), BF16),
            jax.ShapeDtypeStruct((b, ATTN_KV_HEADS, s, ATTN_HEAD_DIM), BF16),
            jax.ShapeDtypeStruct((b, nkv, s), BF16),
        ],
        grid=(b, s // bm),
        in_specs=[
            pl.BlockSpec((None, bm, d), lambda i, j: (i, j, 0)),
            _resident((1, d)),
            rows(nq, 0),
            rows(nkv, (nq + nkv) // nkv),
            _resident((d, ATTN_KV_HEADS * LANES)),
            _resident((1, ATTN_HEAD_DIM)),
        ],
        out_specs=[
            pl.BlockSpec((None, nq, bm), lambda i, j: (i, 0, j)),
            pl.BlockSpec((None, ATTN_KV_HEADS, bm, ATTN_HEAD_DIM), lambda i, j: (i, 0, j, 0)),
            pl.BlockSpec((None, nkv, bm), lambda i, j: (i, 0, j)),
        ],
        name="attn_qkv",
    )


def _attn_core_body(qT_ref, kp_ref, ko_ref, vTp_ref, vTo_ref, sink_ref, wo_ref, x_ref, out_ref, o_scr,
                    *, steps_per_seq, n_steps):
    t = pl.program_id(0)
    first_of_seq = jnp.minimum(t, n_steps - 1) % steps_per_seq == 0

    @pl.when(t == 0)
    def _():
        o_scr[...] = jnp.zeros(o_scr.shape, BF16)

    qb, hd, ck, grp = ATTN_BLOCK, ATTN_HEAD_DIM, CHUNK, ATTN_GROUP
    nvis = 3 * ck
    low = lax.broadcasted_iota(jnp.int32, (hd, LANES), 1) < ck
    no_prev = jnp.where(first_of_seq, NEG, 0.0)
    zeros = jnp.zeros((ck, grp * ck), BF16)
    nchunk = qb // ck

    def band(a, h):
        rows = slice(h * hd, (h + 1) * hd)
        own = slice(a * qb, (a + 1) * qb)
        if a == 0:
            k_prev, v_prev = kp_ref[h], vTp_ref[rows, :]
        else:
            k_prev, v_prev = ko_ref[h, (a - 1) * qb:a * qb, :], vTo_ref[rows, (a - 1) * qb:a * qb]
        return (jnp.concatenate([k_prev, ko_ref[h, own, :]], axis=0),
                jnp.concatenate([v_prev, vTo_ref[rows, own]], axis=1))

    def scores(a, h, c):
        tiles = [qT_ref[(h * grp + g) * hd:(h * grp + g + 1) * hd, a * qb:(a + 1) * qb] for g in range(grp)]
        if c == 0:
            pairs = [jnp.where(low, tiles[2 * j], pltpu.roll(tiles[2 * j + 1], ck, axis=1)) for j in range(grp // 2)]
        else:
            pairs = [jnp.where(low, pltpu.roll(tiles[2 * j], ck, axis=1), tiles[2 * j + 1]) for j in range(grp // 2)]
        q = jnp.concatenate(pairs, axis=1)
        kb, _ = band(a, h)
        return jnp.dot(kb[c * ck:c * ck + nvis, :], q, preferred_element_type=F32)

    def attend(a, h, c, s):
        if a == 0:
            nprev = qb - c * ck
            s = jnp.concatenate([s[:nprev] + no_prev, s[nprev:]], axis=0)
        sk = sink_ref[h]
        m = jnp.maximum(jnp.max(s, axis=0, keepdims=True), sk)
        p = jnp.exp2(s - m)
        l = jnp.sum(p, axis=0, keepdims=True) + jnp.exp2(sk - m)
        pb = p.astype(BF16)
        pb = jnp.concatenate([pb, zeros] if c == 0 else [zeros, pb], axis=0)
        _, vb = band(a, h)
        return (jnp.dot(vb, pb, preferred_element_type=F32) / l).astype(BF16)

    subs = [(a, h, c) for a in range(ATTN_STEP_BLOCKS) for h in range(ATTN_KV_HEADS) for c in range(nchunk)]
    d_model = wo_ref.shape[1]
    n_proj = len(subs) // ATTN_STEP_BLOCKS
    pw = d_model // n_proj

    def project(i):
        cols = slice(i * pw, (i + 1) * pw)
        out_ref[:, cols] = x_ref[:, cols] + jnp.dot(o_scr[...], wo_ref[:, cols], preferred_element_type=F32)

    ahead = ATTN_SCORE_LOOKAHEAD
    pending = [scores(*sub) for sub in subs[:ahead]]
    outs = {}
    new_tiles = []
    for i, (a, h, c) in enumerate(subs):
        s_cur = pending.pop(0)
        if i + ahead < len(subs):
            pending.append(scores(*subs[i + ahead]))
        outs[c] = attend(a, h, c, s_cur)
        if i % ATTN_STEP_BLOCKS == ATTN_STEP_BLOCKS - 1:
            project(i // ATTN_STEP_BLOCKS)
        if c == nchunk - 1:
            for j in range(grp // 2):
                a0 = outs[0][:, j * LANES:(j + 1) * LANES]
                a1 = outs[1][:, j * LANES:(j + 1) * LANES]
                even = jnp.where(low, a0, pltpu.roll(a1, ck, axis=1))
                odd = jnp.where(low, pltpu.roll(a0, ck, axis=1), a1)
                new_tiles.append((a, h * (grp // 2) + j, jnp.concatenate([even, odd], axis=0).T))
    for a, p_idx, tile in new_tiles:
        o_scr[a * qb:(a + 1) * qb, p_idx * LANES:(p_idx + 1) * LANES] = tile


def _attn_core(qT, k, vT, sink_rows, w_o, x2d):
    b, nq, s = qT.shape
    nkv = vT.shape[1]
    d = w_o.shape[1]
    qb = ATTN_BLOCK
    sb = ATTN_STEP_BLOCKS * qb
    per_seq = s // sb
    total = b * per_seq
    cur = lambda t: jnp.minimum(t, total - 1)
    seq = lambda t: cur(t) // per_seq
    grp_i = lambda t: cur(t) % per_seq
    prev = lambda t: jnp.maximum(grp_i(t) * ATTN_STEP_BLOCKS - 1, 0)
    done = lambda t: jnp.maximum(t - 1, 0)
    return pl.pallas_call(
        functools.partial(_attn_core_body, steps_per_seq=per_seq, n_steps=total),
        out_shape=jax.ShapeDtypeStruct((b * s, d), F32),
        grid=(total + 1,),
        in_specs=[
            pl.BlockSpec((None, nq, sb), lambda t: (seq(t), 0, grp_i(t))),
            pl.BlockSpec((None, ATTN_KV_HEADS, qb, ATTN_HEAD_DIM), lambda t: (seq(t), 0, prev(t), 0)),
            pl.BlockSpec((None, ATTN_KV_HEADS, sb, ATTN_HEAD_DIM), lambda t: (seq(t), 0, grp_i(t), 0)),
            pl.BlockSpec((None, nkv, qb), lambda t: (seq(t), 0, prev(t))),
            pl.BlockSpec((None, nkv, sb), lambda t: (seq(t), 0, grp_i(t))),
            _resident((ATTN_KV_HEADS, 1, ATTN_GROUP * CHUNK)),
            _resident((nq, d)),
            pl.BlockSpec((sb, d), lambda t: (done(t), 0)),
        ],
        out_specs=pl.BlockSpec((sb, d), lambda t: (done(t), 0)),
        scratch_shapes=[pltpu.VMEM((sb, nq), BF16)],
        compiler_params=_params("arbitrary"),
        name="attn_core",
    )(qT, k, k, vT, vT, sink_rows, w_o, x2d)


def _attention_layer(x3d, mix_norm, w_qkv, q_norm, k_norm, sinks, w_o, cast=()):
    b, s, d = x3d.shape
    nq = ATTN_HEADS * ATTN_HEAD_DIM
    nkv = ATTN_KV_HEADS * ATTN_HEAD_DIM
    wk = w_qkv[:, nq:nq + nkv]
    w_qkvT = w_qkv.T.astype(BF16)
    wk_pad = jnp.pad(wk.reshape(d, ATTN_KV_HEADS, ATTN_HEAD_DIM),
                     ((0, 0), (0, 0), (0, LANES - ATTN_HEAD_DIM))).reshape(d, ATTN_KV_HEADS * LANES).astype(BF16)
    kgain = (q_norm * k_norm).astype(F32) * LOG2E
    sink_rows = jnp.repeat(sinks.astype(F32).reshape(ATTN_KV_HEADS, 1, ATTN_GROUP) * LOG2E, CHUNK, axis=2)
    (qT, k, vT), (w_o_bf16, *casted) = _attn_qkv(x3d, mix_norm, w_qkvT, wk_pad, kgain, (w_o,) + tuple(cast))
    out = _attn_core(qT, k, vT, sink_rows, w_o_bf16, x3d.reshape(b * s, d))
    return out.reshape(b, s, d), casted


def _conv_body(x_ref, g_ref, win_ref, bin_ref, dw_ref, dwb_ref, lng_ref, lnb_ref, wout_ref, bout_ref,
               o_ref, gbuf, cbuf, ybuf):
    bm, d = x_ref.shape
    nslab = d // LANES

    @pl.when(pl.program_id(1) == 0)
    def _():
        gbuf[:, 0:CONV_HALO, :] = jnp.zeros((nslab, CONV_HALO, LANES), F32)

    x = x_ref[...]
    h = _rms_norm_rows(x, g_ref[...]).astype(BF16)
    u = jnp.dot(h, win_ref[...], preferred_element_type=F32) + bin_ref[...]
    glu = u[:, :d] * jax.nn.sigmoid(u[:, d:])
    for j in range(nslab):
        gbuf[j, CONV_HALO:CONV_HALO + bm, :] = glu[:, j * LANES:(j + 1) * LANES]

    def slab(j, carry):
        acc = jnp.broadcast_to(dwb_ref[j], (bm, LANES))
        taps = dw_ref[j]
        for t in range(CONV_WIDTH):
            r0 = CONV_HALO - (CONV_WIDTH - 1) + t
            acc = acc + taps[t:t + 1, :] * gbuf[j, pl.ds(r0, bm), :]
        cbuf[j] = acc
        gbuf[j, 0:CONV_HALO, :] = gbuf[j, pl.ds(bm, CONV_HALO), :]
        return carry

    lax.fori_loop(0, nslab, slab, 0)

    tot = cbuf[0]
    for j in range(1, nslab):
        tot = tot + cbuf[j]
    mu = jnp.sum(tot, axis=-1, keepdims=True) * (1.0 / d)
    sq = None
    for j in range(nslab):
        c = cbuf[j] - mu
        sq = c * c if sq is None else sq + c * c
    inv = lax.rsqrt(jnp.sum(sq, axis=-1, keepdims=True) * (1.0 / d) + EPS)
    for j in range(nslab):
        y = (cbuf[j] - mu) * inv * lng_ref[:, j * LANES:(j + 1) * LANES] + lnb_ref[:, j * LANES:(j + 1) * LANES]
        ybuf[:, j * LANES:(j + 1) * LANES] = (y * jax.nn.sigmoid(y)).astype(BF16)
    o_ref[...] = x + jnp.dot(ybuf[...], wout_ref[...], preferred_element_type=F32) + bout_ref[...]


def _conv_layer(x3d, mix_norm, w_in, b_in, dw, dw_b, ln_g, ln_b, w_out, b_out, cast=(), *, bm=256):
    b, s, d = x3d.shape
    nslab = d // LANES
    dw_slabs = dw.astype(F32).reshape(CONV_WIDTH, nslab, LANES).transpose(1, 0, 2)
    dwb_slabs = dw_b.astype(F32).reshape(nslab, 1, LANES)
    row = lambda v: v.astype(F32).reshape(1, -1)
    (out,), casted = _call_with_casts(
        _conv_body,
        (x3d, row(mix_norm), w_in.astype(BF16), row(b_in), dw_slabs, dwb_slabs, row(ln_g), row(ln_b),
         w_out.astype(BF16), row(b_out)),
        cast,
        out_shape=[jax.ShapeDtypeStruct((b, s, d), F32)],
        grid=(b, s // bm),
        in_specs=[
            pl.BlockSpec((None, bm, d), lambda i, j: (i, j, 0)),
            _resident((1, d)),
            _resident((d, 2 * d)),
            _resident((1, 2 * d)),
            _resident((nslab, CONV_WIDTH, LANES)),
            _resident((nslab, 1, LANES)),
            _resident((1, d)),
            _resident((1, d)),
            _resident((d, d)),
            _resident((1, d)),
        ],
        out_specs=[pl.BlockSpec((None, bm, d), lambda i, j: (i, j, 0))],
        scratch_shapes=[
            pltpu.VMEM((nslab, CONV_HALO + bm, LANES), F32),
            pltpu.VMEM((nslab, bm, LANES), F32),
            pltpu.VMEM((bm, d), BF16),
        ],
        name="conv_module",
    )
    return out, casted


def _mlstm_proj_body(x_ref, g_ref, wq_ref, wk_ref, wv_ref, wo_ref, wg_ref, bo_ref, bg_ref,
                     q_ref, kT_ref, v_ref, og_ref, gi_ref, gf_ref):
    h = _rms_norm_rows(x_ref[...], g_ref[...]).astype(BF16)
    q = jnp.dot(h, wq_ref[...], preferred_element_type=F32)
    q_ref[...] = (q * (MLSTM_QK_DIM ** -0.5)).astype(BF16)
    kT_ref[...] = jnp.dot(h, wk_ref[...], preferred_element_type=F32).astype(BF16).T
    v_ref[...] = jnp.dot(h, wv_ref[...], preferred_element_type=F32).astype(BF16)
    o_pre = jnp.dot(h, wo_ref[...], preferred_element_type=F32) + bo_ref[...]
    og_ref[...] = jax.nn.sigmoid(o_pre).astype(BF16)
    gates = jnp.dot(h, wg_ref[...], preferred_element_type=F32) + bg_ref[...]
    gi_ref[...] = gates[:, :LANES]
    gf_ref[...] = gates[:, LANES:]


def _mlstm_proj(x3d, gain, w_in, wg, bo, bg, cast=(), *, bm=256):
    b, s, d = x3d.shape
    nqk = MLSTM_HEADS * MLSTM_QK_DIM
    tok = lambda n: pl.BlockSpec((None, bm, n), lambda i, j: (i, j, 0))
    cols = lambda n, c: pl.BlockSpec((d, n), lambda i, j: (0, c), pipeline_mode=pl.Buffered(1))
    assert d == 2 * nqk
    return _call_with_casts(
        _mlstm_proj_body,
        (x3d, gain.reshape(1, d), w_in, w_in, w_in, w_in, wg, bo, bg),
        cast,
        out_shape=[
            jax.ShapeDtypeStruct((b, s, nqk), BF16),
            jax.ShapeDtypeStruct((b, nqk, s), BF16),
            jax.ShapeDtypeStruct((b, s, d), BF16),
            jax.ShapeDtypeStruct((b, s, d), BF16),
            jax.ShapeDtypeStruct((b, s, LANES), F32),
            jax.ShapeDtypeStruct((b, s, LANES), F32),
        ],
        grid=(b, s // bm),
        in_specs=[
            tok(d),
            _resident((1, d)),
            cols(nqk, 0),
            cols(nqk, 1),
            cols(d, 1),
            cols(d, 2),
            _resident((d, 2 * LANES)),
            _resident((1, d)),
            _resident((1, 2 * LANES)),
        ],
        out_specs=[
            tok(nqk),
            pl.BlockSpec((None, nqk, bm), lambda i, j: (i, 0, j)),
            tok(d),
            tok(d),
            tok(LANES),
            tok(LANES),
        ],
        name="mlstm_proj",
    )


def _scan_rows(x, op, fill, period):
    r = lax.broadcasted_iota(jnp.int32, x.shape, 0) % period
    k = 1
    while k < period:
        x = op(x, jnp.where(r >= k, pltpu.roll(x, k, axis=0), fill))
        k *= 2
    return x


def _mlstm_gates_body(gi_ref, gf_ref, negm_ref, winter_ref, floor_ref, u_ref, kw_ref, dec_ref):
    s = gi_ref.shape[0]
    L = MLSTM_CHUNK
    nc = s // L
    gf = gf_ref[...]
    logf = jnp.minimum(gf, 0.0) - jnp.log(1.0 + jnp.exp(-jnp.abs(gf)))
    b = _scan_rows(logf, jnp.add, 0.0, L)
    u = gi_ref[...] - b
    cm = _scan_rows(u, jnp.maximum, -jnp.inf, L)
    m = jnp.zeros((1, LANES), F32)
    for c in range(nc):
        rows = slice(c * L, (c + 1) * L)
        cm_c = cm[rows]
        b_c = b[rows]
        u_c = u[rows]
        big_m = jnp.maximum(m, cm_c)
        m_last = big_m[L - 1:L, :]
        negm_ref[:, rows] = (-big_m).T[:MLSTM_HEADS]
        winter_ref[:, rows] = jnp.exp(m - big_m).T[:MLSTM_HEADS]
        floor_ref[:, rows] = jnp.exp(-(b_c + big_m)).T[:MLSTM_HEADS]
        uT = u_c.T
        mlT = jnp.broadcast_to(m_last, (L, LANES)).T
        mpT = jnp.broadcast_to(m, (L, LANES)).T
        u_ref[:, rows] = uT[:MLSTM_HEADS]
        kw_ref[:, rows] = jnp.exp(uT - mlT)[:MLSTM_HEADS]
        dec_ref[:, rows] = jnp.exp(mpT - mlT)[:MLSTM_HEADS]
        m = b_c[L - 1:L, :] + m_last


def _mlstm_gates(gi, gf):
    b, s, _ = gi.shape
    hrow = jax.ShapeDtypeStruct((b, MLSTM_HEADS, s), F32)
    tok = pl.BlockSpec((None, s, LANES), lambda i: (i, 0, 0))
    rowspec = pl.BlockSpec((None, MLSTM_HEADS, s), lambda i: (i, 0, 0))
    return pl.pallas_call(
        _mlstm_gates_body,
        out_shape=(hrow,) * 6,
        grid=(b,),
        in_specs=[tok, tok],
        out_specs=(rowspec,) * 6,
        compiler_params=_params("parallel"),
        name="mlstm_gates",
    )(gi, gf)


def _mlstm_core_body(q_ref, kT_ref, v_ref, og_ref, hn_ref, negm_ref, winter_ref, floor_ref,
                     u_ref, kw_ref, dec_ref, o_ref):
    L = MLSTM_CHUNK
    dk, dv = MLSTM_QK_DIM, MLSTM_V_DIM
    nc = q_ref.shape[0] // L
    tril = lax.broadcasted_iota(jnp.int32, (L, L), 0) >= lax.broadcasted_iota(jnp.int32, (L, L), 1)

    def state_free(c):
        rows = slice(c * L, (c + 1) * L)
        kT = kT_ref[:, rows]
        qk = jnp.dot(q_ref[rows, :], kT, preferred_element_type=F32)
        kwT = kT.astype(F32) * kw_ref[:, rows]
        inc = jnp.dot(kwT.astype(BF16), v_ref[rows, :], preferred_element_type=F32)
        return qk, inc, jnp.sum(kwT, axis=-1, keepdims=True)

    cmat = jnp.zeros((dk, dv), F32)
    nvec = jnp.zeros((dk, LANES), F32)
    pending = [state_free(c) for c in range(min(MLSTM_LOOKAHEAD, nc))]
    for c in range(nc):
        rows = slice(c * L, (c + 1) * L)
        qk, inc, ksum = pending.pop(0)
        if c + MLSTM_LOOKAHEAD < nc:
            pending.append(state_free(c + MLSTM_LOOKAHEAD))
        q = q_ref[rows, :]
        column = lambda ref: jnp.broadcast_to(ref[:, rows], (LANES, L)).T
        wint = column(winter_ref)
        inter = jnp.dot(q, cmat.astype(BF16), preferred_element_type=F32)
        qn_inter = jnp.dot(q, nvec.astype(BF16), preferred_element_type=F32)
        dmat = jnp.where(tril, jnp.exp(u_ref[:, rows] + column(negm_ref)), 0.0)
        p = qk * dmat
        num = jnp.dot(p.astype(BF16), v_ref[rows, :], preferred_element_type=F32)
        num = num + jnp.concatenate([wint, wint], axis=1) * inter
        qn = jnp.sum(p, axis=-1, keepdims=True) + wint * qn_inter
        den = jnp.maximum(jnp.abs(qn), column(floor_ref))
        r = 1.0 / den
        scale = r * lax.rsqrt(jnp.mean(num * num, axis=-1, keepdims=True) * r * r + EPS)
        y = num * jnp.concatenate([scale, scale], axis=1) * hn_ref[...]
        o_ref[rows, :] = (og_ref[rows, :].astype(F32) * y).astype(BF16)

        dec = dec_ref[:, rows]
        cmat = jnp.concatenate([dec, dec], axis=1) * cmat + inc
        nvec = dec * nvec + ksum


def _mlstm_core(q, kT, v, og, h_norm, negm, winter, floor, u, kw, dec):
    b, s, _ = q.shape
    dk, dv = MLSTM_QK_DIM, MLSTM_V_DIM
    d = MLSTM_HEADS * dv
    rowspec = pl.BlockSpec((None, None, 1, s), lambda i, j: (i, j, 0, 0))
    per_head_row = lambda a: a.reshape(b, MLSTM_HEADS, 1, s)
    return pl.pallas_call(
        _mlstm_core_body,
        out_shape=jax.ShapeDtypeStruct((b, s, d), BF16),
        grid=(b, MLSTM_HEADS),
        in_specs=[
            pl.BlockSpec((None, s, dk), lambda i, j: (i, 0, j)),
            pl.BlockSpec((None, dk, s), lambda i, j: (i, j, 0)),
            pl.BlockSpec((None, s, dv), lambda i, j: (i, 0, j)),
            pl.BlockSpec((None, s, dv), lambda i, j: (i, 0, j)),
            pl.BlockSpec((1, dv), lambda i, j: (0, j)),
            rowspec, rowspec, rowspec, rowspec, rowspec, rowspec,
        ],
        out_specs=pl.BlockSpec((None, s, dv), lambda i, j: (i, 0, j)),
        compiler_params=_params("parallel", "parallel"),
        name="mlstm_core",
    )(q, kT, v, og, h_norm.reshape(1, d), *(per_head_row(a) for a in (negm, winter, floor, u, kw, dec)))


def _mlstm_layer(x3d, mix_norm, w_in, b_gates, h_norm, w_out, cast=()):
    b, s, d = x3d.shape
    nh = MLSTM_HEADS
    nqk = nh * MLSTM_QK_DIM
    w_in_bf16 = w_in.astype(BF16)
    wg = w_in[:, 2 * nqk + 2 * d:]
    pad = lambda w: jnp.pad(w, ((0, 0), (0, LANES - nh)))
    w_gates = jnp.concatenate([pad(wg[:, :nh]), pad(wg[:, nh:])], axis=1).astype(BF16)
    bg = b_gates[d:].astype(F32).reshape(1, 2 * nh)
    b_gates_row = jnp.concatenate([pad(bg[:, :nh]), pad(bg[:, nh:])], axis=1)
    bo = b_gates[:d].astype(F32).reshape(1, d)
    (q, kT, v, og, gi, gf), (w_out_bf16, *casted) = _mlstm_proj(
        x3d, mix_norm, w_in_bf16, w_gates, bo, b_gates_row, (w_out,) + tuple(cast))
    negm, winter, floor, u, kw, dec = _mlstm_gates(gi, gf)
    hgated = _mlstm_core(q, kT, v, og, h_norm.astype(F32), negm, winter, floor, u, kw, dec)
    out = _proj_res(hgated.reshape(b * s, d), w_out_bf16, jnp.zeros((d,), F32), x3d.reshape(b * s, d))
    return out.reshape(b, s, d), casted


def kernel(x, l0_mix_norm, l0_attn_w_qkv, l0_attn_q_norm, l0_attn_k_norm, l0_attn_sinks, l0_attn_w_o, l0_mlp_norm, l0_mlp_w1, l0_mlp_w2, l1_mix_norm, l1_conv_w_in, l1_conv_b_in, l1_conv_dw, l1_conv_dw_b, l1_conv_ln_g, l1_conv_ln_b, l1_conv_w_out, l1_conv_b_out, l1_mlp_norm, l1_mlp_w1, l1_mlp_w2, l2_mix_norm, l2_mlstm_w_in, l2_mlstm_b_gates, l2_mlstm_h_norm, l2_mlstm_w_out, l2_mlp_norm, l2_mlp_w1, l2_mlp_w2, l3_mix_norm, l3_attn_w_qkv, l3_attn_q_norm, l3_attn_k_norm, l3_attn_sinks, l3_attn_w_o, l3_mlp_norm, l3_mlp_w1, l3_mlp_w2):
    b, s, d = x.shape

    def mlp(x3d, norm, w1, w2):
        return _mlp(x3d.reshape(b * s, d), norm.astype(F32), w1, w2).reshape(b, s, d)

    x, (w1, w2, conv_w_in, conv_w_out) = _attention_layer(
        x, l0_mix_norm, l0_attn_w_qkv, l0_attn_q_norm, l0_attn_k_norm, l0_attn_sinks, l0_attn_w_o,
        cast=(l0_mlp_w1, l0_mlp_w2, l1_conv_w_in, l1_conv_w_out))
    x = mlp(x, l0_mlp_norm, w1, w2)
    x, (w1, w2) = _conv_layer(
        x, l1_mix_norm, conv_w_in, l1_conv_b_in, l1_conv_dw, l1_conv_dw_b, l1_conv_ln_g, l1_conv_ln_b, conv_w_out,
        l1_conv_b_out, cast=(l1_mlp_w1, l1_mlp_w2))
    x = mlp(x, l1_mlp_norm, w1, w2)
    x, (w1, w2) = _mlstm_layer(x, l2_mix_norm, l2_mlstm_w_in, l2_mlstm_b_gates, l2_mlstm_h_norm, l2_mlstm_w_out,
                               cast=(l2_mlp_w1, l2_mlp_w2))
    x = mlp(x, l2_mlp_norm, w1, w2)
    x, (w1, w2) = _attention_layer(
        x, l3_mix_norm, l3_attn_w_qkv, l3_attn_q_norm, l3_attn_k_norm, l3_attn_sinks, l3_attn_w_o,
        cast=(l3_mlp_w1, l3_mlp_w2))
    x = mlp(x, l3_mlp_norm, w1, w2)
    return x
```

```python
import functools

import jax
import jax.numpy as jnp
from jax import lax
from jax.experimental import pallas as pl
from jax.experimental.pallas import tpu as pltpu

F32 = jnp.float32
BF16 = jnp.bfloat16

EPS = 1e-6
NEG = -1e30
LOG2E = 1.4426950408889634

ATTN_HEADS = 32
ATTN_KV_HEADS = 4
ATTN_GROUP = ATTN_HEADS // ATTN_KV_HEADS
ATTN_HEAD_DIM = 64
ATTN_BLOCK = 128
CHUNK = 64
CONV_WIDTH = 31
MLSTM_HEADS = 8
MLSTM_QK_DIM = 128
MLSTM_V_DIM = 256
MLSTM_CHUNK = 128

LANES = 128
VMEM_LIMIT_BYTES = 60000 * 1024

BF16_SUBLANES = 16

QKV_ROW_SPLIT = 2
ATTN_STEP_BLOCKS = 4
ATTN_SCORE_LOOKAHEAD = 2
MLSTM_LOOKAHEAD = 1
CONV_HALO = 32


def _params(*sem):
    return pltpu.CompilerParams(dimension_semantics=sem, vmem_limit_bytes=VMEM_LIMIT_BYTES)


def _resident(shape):
    nd = len(shape)
    return pl.BlockSpec(shape, lambda *_: (0,) * nd, pipeline_mode=pl.Buffered(1))


def _rms_norm_rows(x, gain):
    return x * lax.rsqrt(jnp.mean(x * x, axis=-1, keepdims=True) + EPS) * gain


def _cast_specs(w, grid):
    rows, cols = w.shape
    steps = 1
    for g in grid:
        steps *= g
    per_step = rows // steps
    assert per_step * steps == rows and per_step >= 1
    blk = max(per_step, BF16_SUBLANES)
    share = blk // per_step

    def index_map(*g):
        flat = g[0]
        for n, i in zip(grid[1:], g[1:]):
            flat = flat * n + i
        return flat // share, 0

    return pl.BlockSpec((blk, cols), index_map)


def _call_with_casts(body, args, cast, *, grid, in_specs, out_specs, out_shape, scratch_shapes=(), name):
    n_in, n_out, n_cast = len(in_specs), len(out_specs), len(cast)

    def wrapped(*refs):
        cast_in = refs[n_in:n_in + n_cast]
        cast_out = refs[n_in + n_cast + n_out:n_in + 2 * n_cast + n_out]
        for src, dst in zip(cast_in, cast_out):
            dst[...] = src[...].astype(BF16)
        body(*refs[:n_in], *refs[n_in + n_cast:n_in + n_cast + n_out], *refs[n_in + 2 * n_cast + n_out:])

    side = [_cast_specs(w, grid) for w in cast]
    res = pl.pallas_call(
        wrapped,
        out_shape=list(out_shape) + [jax.ShapeDtypeStruct(w.shape, BF16) for w in cast],
        grid=grid,
        in_specs=list(in_specs) + side,
        out_specs=list(out_specs) + side,
        scratch_shapes=list(scratch_shapes),
        compiler_params=_params(*(("arbitrary",) * len(grid))),
        name=name,
    )(*args, *cast)
    return res[:n_out], res[n_out:]


def _mlp_body(x_ref, g_ref, w1_ref, w2_ref, o_ref, h_ref):
    def up_down(h):
        a = jnp.dot(h, w1_ref[...], preferred_element_type=F32)
        a = jnp.square(jnp.maximum(a, 0.0)).astype(BF16)
        return jnp.dot(a, w2_ref[...], preferred_element_type=F32)

    @pl.when(pl.program_id(1) == 0)
    def _():
        x = x_ref[...]
        h = _rms_norm_rows(x, g_ref[...]).astype(BF16)
        h_ref[...] = h
        o_ref[...] = x + up_down(h)

    @pl.when(pl.program_id(1) != 0)
    def _():
        o_ref[...] += up_down(h_ref[...])


def _mlp(x2d, gain, w1, w2, *, bm=1024, fc=1024):
    m, d = x2d.shape
    ff = w1.shape[1]
    return pl.pallas_call(
        _mlp_body,
        out_shape=jax.ShapeDtypeStruct((m, d), F32),
        grid=(m // bm, ff // fc),
        in_specs=[
            pl.BlockSpec((bm, d), lambda i, j: (i, 0)),
            pl.BlockSpec((1, d), lambda i, j: (0, 0)),
            pl.BlockSpec((d, fc), lambda i, j: (0, j)),
            pl.BlockSpec((fc, d), lambda i, j: (j, 0)),
        ],
        out_specs=pl.BlockSpec((bm, d), lambda i, j: (i, 0)),
        scratch_shapes=[pltpu.VMEM((bm, d), BF16)],
        compiler_params=_params("parallel", "arbitrary"),
        name="mlp",
    )(x2d, gain.reshape(1, d), w1, w2)


def _proj_res_body(a_ref, w_ref, x_ref, o_ref):
    o_ref[...] = x_ref[...] + jnp.dot(a_ref[...], w_ref[...], preferred_element_type=F32)


def _proj_res(a2d, w, x2d, *, bm=512):
    m, k = a2d.shape
    d = w.shape[1]
    return pl.pallas_call(
        _proj_res_body,
        out_shape=jax.ShapeDtypeStruct((m, d), F32),
        grid=(m // bm,),
        in_specs=[
            pl.BlockSpec((bm, k), lambda i: (i, 0)),
            _resident((k, d)),
            pl.BlockSpec((bm, d), lambda i: (i, 0)),
        ],
        out_specs=pl.BlockSpec((bm, d), lambda i: (i, 0)),
        compiler_params=_params("parallel"),
        name="proj_res",
    )(a2d, w, x2d)


def _attn_qkv_body(x_ref, g_ref, wqT_ref, wvT_ref, wk_ref, kgain_ref, qT_ref, k_ref, vT_ref):
    nq = ATTN_HEADS * ATTN_HEAD_DIM
    bm = x_ref.shape[0]
    rt = bm // QKV_ROW_SPLIT
    nt = (((1,), (1,)), ((), ()))
    prods = []
    wqvT = jnp.concatenate([wqT_ref[...], wvT_ref[...]], axis=0)
    for r in range(QKV_ROW_SPLIT):
        h = _rms_norm_rows(x_ref[r * rt:(r + 1) * rt, :], g_ref[...]).astype(BF16)
        qvT = lax.dot_general(wqvT, h, nt, preferred_element_type=F32)
        kf = jnp.dot(h, wk_ref[...], preferred_element_type=F32)
        prods.append((qvT[:nq], qvT[nq:], kf))
    for r, (qT, vT, kf) in enumerate(prods):
        q3 = qT.reshape(ATTN_HEADS, ATTN_HEAD_DIM, rt)
        inv = lax.rsqrt(jnp.mean(q3 * q3, axis=1, keepdims=True) + EPS) * (ATTN_HEAD_DIM ** -0.5)
        qT_ref[:, r * rt:(r + 1) * rt] = (q3 * inv).reshape(nq, rt).astype(BF16)
        vT_ref[:, r * rt:(r + 1) * rt] = vT.astype(BF16)
        for j in range(ATTN_KV_HEADS):
            kj = kf[:, j * LANES:j * LANES + ATTN_HEAD_DIM]
            kn = kj * lax.rsqrt(jnp.mean(kj * kj, axis=-1, keepdims=True) + EPS)
            k_ref[j, r * rt:(r + 1) * rt, :] = (kn * kgain_ref[...]).astype(BF16)


def _attn_qkv(x3d, gain, w_qkvT, wk_pad, kgain, cast=(), *, bm=512):
    b, s, d = x3d.shape
    nq = ATTN_HEADS * ATTN_HEAD_DIM
    nkv = ATTN_KV_HEADS * ATTN_HEAD_DIM
    assert nq % nkv == 0
    rows = lambda n, r: pl.BlockSpec((n, d), lambda i, j: (r, 0), pipeline_mode=pl.Buffered(1))
    return _call_with_casts(
        _attn_qkv_body,
        (x3d, gain.reshape(1, d), w_qkvT, w_qkvT, wk_pad, kgain.reshape(1, ATTN_HEAD_DIM)),
        cast,
        out_shape=[
            jax.ShapeDtypeStruct((b, nq, s), BF16),
            jax.ShapeDtypeStruct((b, ATTN_KV_HEADS, s, ATTN_HEAD_DIM), BF16),
            jax.ShapeDtypeStruct((b, nkv, s), BF16),
        ],
        grid=(b, s // bm),
        in_specs=[
            pl.BlockSpec((None, bm, d), lambda i, j: (i, j, 0)),
            _resident((1, d)),
            rows(nq, 0),
            rows(nkv, (nq + nkv) // nkv),
            _resident((d, ATTN_KV_HEADS * LANES)),
            _resident((1, ATTN_HEAD_DIM)),
        ],
        out_specs=[
            pl.BlockSpec((None, nq, bm), lambda i, j: (i, 0, j)),
            pl.BlockSpec((None, ATTN_KV_HEADS, bm, ATTN_HEAD_DIM), lambda i, j: (i, 0, j, 0)),
            pl.BlockSpec((None, nkv, bm), lambda i, j: (i, 0, j)),
        ],
        name="attn_qkv",
    )


def _attn_core_body(qT_ref, kp_ref, ko_ref, vTp_ref, vTo_ref, sink_ref, wo_ref, x_ref, out_ref, o_scr,
                    *, steps_per_seq, n_steps):
    t = pl.program_id(0)
    first_of_seq = jnp.minimum(t, n_steps - 1) % steps_per_seq == 0

    @pl.when(t == 0)
    def _():
        o_scr[...] = jnp.zeros(o_scr.shape, BF16)

    qb, hd, ck, grp = ATTN_BLOCK, ATTN_HEAD_DIM, CHUNK, ATTN_GROUP
    nvis = 3 * ck
    low = lax.broadcasted_iota(jnp.int32, (hd, LANES), 1) < ck
    no_prev = jnp.where(first_of_seq, NEG, 0.0)
    zeros = jnp.zeros((ck, grp * ck), BF16)
    nchunk = qb // ck

    def band(a, h):
        rows = slice(h * hd, (h + 1) * hd)
        own = slice(a * qb, (a + 1) * qb)
        if a == 0:
            k_prev, v_prev = kp_ref[h], vTp_ref[rows, :]
        else:
            k_prev, v_prev = ko_ref[h, (a - 1) * qb:a * qb, :], vTo_ref[rows, (a - 1) * qb:a * qb]
        return (jnp.concatenate([k_prev, ko_ref[h, own, :]], axis=0),
                jnp.concatenate([v_prev, vTo_ref[rows, own]], axis=1))

    def scores(a, h, c):
        tiles = [qT_ref[(h * grp + g) * hd:(h * grp + g + 1) * hd, a * qb:(a + 1) * qb] for g in range(grp)]
        if c == 0:
            pairs = [jnp.where(low, tiles[2 * j], pltpu.roll(tiles[2 * j + 1], ck, axis=1)) for j in range(grp // 2)]
        else:
            pairs = [jnp.where(low, pltpu.roll(tiles[2 * j], ck, axis=1), tiles[2 * j + 1]) for j in range(grp // 2)]
        q = jnp.concatenate(pairs, axis=1)
        kb, _ = band(a, h)
        return jnp.dot(kb[c * ck:c * ck + nvis, :], q, preferred_element_type=F32)

    def attend(a, h, c, s):
        if a == 0:
            nprev = qb - c * ck
            s = jnp.concatenate([s[:nprev] + no_prev, s[nprev:]], axis=0)
        sk = sink_ref[h]
        m = jnp.maximum(jnp.max(s, axis=0, keepdims=True), sk)
        p = jnp.exp2(s - m)
        l = jnp.sum(p, axis=0, keepdims=True) + jnp.exp2(sk - m)
        pb = p.astype(BF16)
        pb = jnp.concatenate([pb, zeros] if c == 0 else [zeros, pb], axis=0)
        _, vb = band(a, h)
        return (jnp.dot(vb, pb, preferred_element_type=F32) / l).astype(BF16)

    subs = [(a, h, c) for a in range(ATTN_STEP_BLOCKS) for h in range(ATTN_KV_HEADS) for c in range(nchunk)]
    d_model = wo_ref.shape[1]
    n_proj = len(subs) // ATTN_STEP_BLOCKS
    pw = d_model // n_proj

    def project(i):
        cols = slice(i * pw, (i + 1) * pw)
        out_ref[:, cols] = x_ref[:, cols] + jnp.dot(o_scr[...], wo_ref[:, cols], preferred_element_type=F32)

    ahead = ATTN_SCORE_LOOKAHEAD
    pending = [scores(*sub) for sub in subs[:ahead]]
    outs = {}
    new_tiles = []
    for i, (a, h, c) in enumerate(subs):
        s_cur = pending.pop(0)
        if i + ahead < len(subs):
            pending.append(scores(*subs[i + ahead]))
        outs[c] = attend(a, h, c, s_cur)
        if i % ATTN_STEP_BLOCKS == ATTN_STEP_BLOCKS - 1:
            project(i // ATTN_STEP_BLOCKS)
        if c == nchunk - 1:
            for j in range(grp // 2):
                a0 = outs[0][:, j * LANES:(j + 1) * LANES]
                a1 = outs[1][:, j * LANES:(j + 1) * LANES]
                even = jnp.where(low, a0, pltpu.roll(a1, ck, axis=1))
                odd = jnp.where(low, pltpu.roll(a0, ck, axis=1), a1)
                new_tiles.append((a, h * (grp // 2) + j, jnp.concatenate([even, odd], axis=0).T))
    for a, p_idx, tile in new_tiles:
        o_scr[a * qb:(a + 1) * qb, p_idx * LANES:(p_idx + 1) * LANES] = tile


def _attn_core(qT, k, vT, sink_rows, w_o, x2d):
    b, nq, s = qT.shape
    nkv = vT.shape[1]
    d = w_o.shape[1]
    qb = ATTN_BLOCK
    sb = ATTN_STEP_BLOCKS * qb
    per_seq = s // sb
    total = b * per_seq
    cur = lambda t: jnp.minimum(t, total - 1)
    seq = lambda t: cur(t) // per_seq
    grp_i = lambda t: cur(t) % per_seq
    prev = lambda t: jnp.maximum(grp_i(t) * ATTN_STEP_BLOCKS - 1, 0)
    done = lambda t: jnp.maximum(t - 1, 0)
    return pl.pallas_call(
        functools.partial(_attn_core_body, steps_per_seq=per_seq, n_steps=total),
        out_shape=jax.ShapeDtypeStruct((b * s, d), F32),
        grid=(total + 1,),
        in_specs=[
            pl.BlockSpec((None, nq, sb), lambda t: (seq(t), 0, grp_i(t))),
            pl.BlockSpec((None, ATTN_KV_HEADS, qb, ATTN_HEAD_DIM), lambda t: (seq(t), 0, prev(t), 0)),
            pl.BlockSpec((None, ATTN_KV_HEADS, sb, ATTN_HEAD_DIM), lambda t: (seq(t), 0, grp_i(t), 0)),
            pl.BlockSpec((None, nkv, qb), lambda t: (seq(t), 0, prev(t))),
            pl.BlockSpec((None, nkv, sb), lambda t: (seq(t), 0, grp_i(t))),
            _resident((ATTN_KV_HEADS, 1, ATTN_GROUP * CHUNK)),
            _resident((nq, d)),
            pl.BlockSpec((sb, d), lambda t: (done(t), 0)),
        ],
        out_specs=pl.BlockSpec((sb, d), lambda t: (done(t), 0)),
        scratch_shapes=[pltpu.VMEM((sb, nq), BF16)],
        compiler_params=_params("arbitrary"),
        name="attn_core",
    )(qT, k, k, vT, vT, sink_rows, w_o, x2d)


def _attention_layer(x3d, mix_norm, w_qkv, q_norm, k_norm, sinks, w_o, cast=()):
    b, s, d = x3d.shape
    nq = ATTN_HEADS * ATTN_HEAD_DIM
    nkv = ATTN_KV_HEADS * ATTN_HEAD_DIM
    wk = w_qkv[:, nq:nq + nkv]
    w_qkvT = w_qkv.T.astype(BF16)
    wk_pad = jnp.pad(wk.reshape(d, ATTN_KV_HEADS, ATTN_HEAD_DIM),
                     ((0, 0), (0, 0), (0, LANES - ATTN_HEAD_DIM))).reshape(d, ATTN_KV_HEADS * LANES).astype(BF16)
    kgain = (q_norm * k_norm).astype(F32) * LOG2E
    sink_rows = jnp.repeat(sinks.astype(F32).reshape(ATTN_KV_HEADS, 1, ATTN_GROUP) * LOG2E, CHUNK, axis=2)
    (qT, k, vT), (w_o_bf16, *casted) = _attn_qkv(x3d, mix_norm, w_qkvT, wk_pad, kgain, (w_o,) + tuple(cast))
    out = _attn_core(qT, k, vT, sink_rows, w_o_bf16, x3d.reshape(b * s, d))
    return out.reshape(b, s, d), casted


def _conv_body(x_ref, g_ref, win_ref, bin_ref, dw_ref, dwb_ref, lng_ref, lnb_ref, wout_ref, bout_ref,
               o_ref, gbuf, cbuf, ybuf):
    bm, d = x_ref.shape
    nslab = d // LANES

    @pl.when(pl.program_id(1) == 0)
    def _():
        gbuf[:, 0:CONV_HALO, :] = jnp.zeros((nslab, CONV_HALO, LANES), F32)

    x = x_ref[...]
    h = _rms_norm_rows(x, g_ref[...]).astype(BF16)
    u = jnp.dot(h, win_ref[...], preferred_element_type=F32) + bin_ref[...]
    glu = u[:, :d] * jax.nn.sigmoid(u[:, d:])
    for j in range(nslab):
        gbuf[j, CONV_HALO:CONV_HALO + bm, :] = glu[:, j * LANES:(j + 1) * LANES]

    def slab(j, carry):
        acc = jnp.broadcast_to(dwb_ref[j], (bm, LANES))
        taps = dw_ref[j]
        for t in range(CONV_WIDTH):
            r0 = CONV_HALO - (CONV_WIDTH - 1) + t
            acc = acc + taps[t:t + 1, :] * gbuf[j, pl.ds(r0, bm), :]
        cbuf[j] = acc
        gbuf[j, 0:CONV_HALO, :] = gbuf[j, pl.ds(bm, CONV_HALO), :]
        return carry

    lax.fori_loop(0, nslab, slab, 0)

    tot = cbuf[0]
    for j in range(1, nslab):
        tot = tot + cbuf[j]
    mu = jnp.sum(tot, axis=-1, keepdims=True) * (1.0 / d)
    sq = None
    for j in range(nslab):
        c = cbuf[j] - mu
        sq = c * c if sq is None else sq + c * c
    inv = lax.rsqrt(jnp.sum(sq, axis=-1, keepdims=True) * (1.0 / d) + EPS)
    for j in range(nslab):
        y = (cbuf[j] - mu) * inv * lng_ref[:, j * LANES:(j + 1) * LANES] + lnb_ref[:, j * LANES:(j + 1) * LANES]
        ybuf[:, j * LANES:(j + 1) * LANES] = (y * jax.nn.sigmoid(y)).astype(BF16)
    o_ref[...] = x + jnp.dot(ybuf[...], wout_ref[...], preferred_element_type=F32) + bout_ref[...]


def _conv_layer(x3d, mix_norm, w_in, b_in, dw, dw_b, ln_g, ln_b, w_out, b_out, cast=(), *, bm=256):
    b, s, d = x3d.shape
    nslab = d // LANES
    dw_slabs = dw.astype(F32).reshape(CONV_WIDTH, nslab, LANES).transpose(1, 0, 2)
    dwb_slabs = dw_b.astype(F32).reshape(nslab, 1, LANES)
    row = lambda v: v.astype(F32).reshape(1, -1)
    (out,), casted = _call_with_casts(
        _conv_body,
        (x3d, row(mix_norm), w_in.astype(BF16), row(b_in), dw_slabs, dwb_slabs, row(ln_g), row(ln_b),
         w_out.astype(BF16), row(b_out)),
        cast,
        out_shape=[jax.ShapeDtypeStruct((b, s, d), F32)],
        grid=(b, s // bm),
        in_specs=[
            pl.BlockSpec((None, bm, d), lambda i, j: (i, j, 0)),
            _resident((1, d)),
            _resident((d, 2 * d)),
            _resident((1, 2 * d)),
            _resident((nslab, CONV_WIDTH, LANES)),
            _resident((nslab, 1, LANES)),
            _resident((1, d)),
            _resident((1, d)),
            _resident((d, d)),
            _resident((1, d)),
        ],
        out_specs=[pl.BlockSpec((None, bm, d), lambda i, j: (i, j, 0))],
        scratch_shapes=[
            pltpu.VMEM((nslab, CONV_HALO + bm, LANES), F32),
            pltpu.VMEM((nslab, bm, LANES), F32),
            pltpu.VMEM((bm, d), BF16),
        ],
        name="conv_module",
    )
    return out, casted


def _mlstm_proj_body(x_ref, g_ref, wq_ref, wk_ref, wv_ref, wo_ref, wg_ref, bo_ref, bg_ref,
                     q_ref, kT_ref, v_ref, og_ref, gi_ref, gf_ref):
    h = _rms_norm_rows(x_ref[...], g_ref[...]).astype(BF16)
    q = jnp.dot(h, wq_ref[...], preferred_element_type=F32)
    q_ref[...] = (q * (MLSTM_QK_DIM ** -0.5)).astype(BF16)
    kT_ref[...] = jnp.dot(h, wk_ref[...], preferred_element_type=F32).astype(BF16).T
    v_ref[...] = jnp.dot(h, wv_ref[...], preferred_element_type=F32).astype(BF16)
    o_pre = jnp.dot(h, wo_ref[...], preferred_element_type=F32) + bo_ref[...]
    og_ref[...] = jax.nn.sigmoid(o_pre).astype(BF16)
    gates = jnp.dot(h, wg_ref[...], preferred_element_type=F32) + bg_ref[...]
    gi_ref[...] = gates[:, :LANES]
    gf_ref[...] = gates[:, LANES:]


def _mlstm_proj(x3d, gain, w_in, wg, bo, bg, cast=(), *, bm=256):
    b, s, d = x3d.shape
    nqk = MLSTM_HEADS * MLSTM_QK_DIM
    tok = lambda n: pl.BlockSpec((None, bm, n), lambda i, j: (i, j, 0))
    cols = lambda n, c: pl.BlockSpec((d, n), lambda i, j: (0, c), pipeline_mode=pl.Buffered(1))
    assert d == 2 * nqk
    return _call_with_casts(
        _mlstm_proj_body,
        (x3d, gain.reshape(1, d), w_in, w_in, w_in, w_in, wg, bo, bg),
        cast,
        out_shape=[
            jax.ShapeDtypeStruct((b, s, nqk), BF16),
            jax.ShapeDtypeStruct((b, nqk, s), BF16),
            jax.ShapeDtypeStruct((b, s, d), BF16),
            jax.ShapeDtypeStruct((b, s, d), BF16),
            jax.ShapeDtypeStruct((b, s, LANES), F32),
            jax.ShapeDtypeStruct((b, s, LANES), F32),
        ],
        grid=(b, s // bm),
        in_specs=[
            tok(d),
            _resident((1, d)),
            cols(nqk, 0),
            cols(nqk, 1),
            cols(d, 1),
            cols(d, 2),
            _resident((d, 2 * LANES)),
            _resident((1, d)),
            _resident((1, 2 * LANES)),
        ],
        out_specs=[
            tok(nqk),
            pl.BlockSpec((None, nqk, bm), lambda i, j: (i, 0, j)),
            tok(d),
            tok(d),
            tok(LANES),
            tok(LANES),
        ],
        name="mlstm_proj",
    )


def _scan_rows(x, op, fill, period):
    r = lax.broadcasted_iota(jnp.int32, x.shape, 0) % period
    k = 1
    while k < period:
        x = op(x, jnp.where(r >= k, pltpu.roll(x, k, axis=0), fill))
        k *= 2
    return x


def _mlstm_gates_body(gi_ref, gf_ref, negm_ref, winter_ref, floor_ref, u_ref, kw_ref, dec_ref):
    s = gi_ref.shape[0]
    L = MLSTM_CHUNK
    nc = s // L
    gf = gf_ref[...]
    logf = jnp.minimum(gf, 0.0) - jnp.log(1.0 + jnp.exp(-jnp.abs(gf)))
    b = _scan_rows(logf, jnp.add, 0.0, L)
    u = gi_ref[...] - b
    cm = _scan_rows(u, jnp.maximum, -jnp.inf, L)
    m = jnp.zeros((1, LANES), F32)
    for c in range(nc):
        rows = slice(c * L, (c + 1) * L)
        cm_c = cm[rows]
        b_c = b[rows]
        u_c = u[rows]
        big_m = jnp.maximum(m, cm_c)
        m_last = big_m[L - 1:L, :]
        negm_ref[:, rows] = (-big_m).T[:MLSTM_HEADS]
        winter_ref[:, rows] = jnp.exp(m - big_m).T[:MLSTM_HEADS]
        floor_ref[:, rows] = jnp.exp(-(b_c + big_m)).T[:MLSTM_HEADS]
        uT = u_c.T
        mlT = jnp.broadcast_to(m_last, (L, LANES)).T
        mpT = jnp.broadcast_to(m, (L, LANES)).T
        u_ref[:, rows] = uT[:MLSTM_HEADS]
        kw_ref[:, rows] = jnp.exp(uT - mlT)[:MLSTM_HEADS]
        dec_ref[:, rows] = jnp.exp(mpT - mlT)[:MLSTM_HEADS]
        m = b_c[L - 1:L, :] + m_last


def _mlstm_gates(gi, gf):
    b, s, _ = gi.shape
    hrow = jax.ShapeDtypeStruct((b, MLSTM_HEADS, s), F32)
    tok = pl.BlockSpec((None, s, LANES), lambda i: (i, 0, 0))
    rowspec = pl.BlockSpec((None, MLSTM_HEADS, s), lambda i: (i, 0, 0))
    return pl.pallas_call(
        _mlstm_gates_body,
        out_shape=(hrow,) * 6,
        grid=(b,),
        in_specs=[tok, tok],
        out_specs=(rowspec,) * 6,
        compiler_params=_params("parallel"),
        name="mlstm_gates",
    )(gi, gf)


def _mlstm_core_body(q_ref, kT_ref, v_ref, og_ref, hn_ref, negm_ref, winter_ref, floor_ref,
                     u_ref, kw_ref, dec_ref, o_ref):
    L = MLSTM_CHUNK
    dk, dv = MLSTM_QK_DIM, MLSTM_V_DIM
    nc = q_ref.shape[0] // L
    tril = lax.broadcasted_iota(jnp.int32, (L, L), 0) >= lax.broadcasted_iota(jnp.int32, (L, L), 1)

    def state_free(c):
        rows = slice(c * L, (c + 1) * L)
        kT = kT_ref[:, rows]
        qk = jnp.dot(q_ref[rows, :], kT, preferred_element_type=F32)
        kwT = kT.astype(F32) * kw_ref[:, rows]
        inc = jnp.dot(kwT.astype(BF16), v_ref[rows, :], preferred_element_type=F32)
        return qk, inc, jnp.sum(kwT, axis=-1, keepdims=True)

    cmat = jnp.zeros((dk, dv), F32)
    nvec = jnp.zeros((dk, LANES), F32)
    pending = [state_free(c) for c in range(min(MLSTM_LOOKAHEAD, nc))]
    for c in range(nc):
        rows = slice(c * L, (c + 1) * L)
        qk, inc, ksum = pending.pop(0)
        if c + MLSTM_LOOKAHEAD < nc:
            pending.append(state_free(c + MLSTM_LOOKAHEAD))
        q = q_ref[rows, :]
        column = lambda ref: jnp.broadcast_to(ref[:, rows], (LANES, L)).T
        wint = column(winter_ref)
        inter = jnp.dot(q, cmat.astype(BF16), preferred_element_type=F32)
        qn_inter = jnp.dot(q, nvec.astype(BF16), preferred_element_type=F32)
        dmat = jnp.where(tril, jnp.exp(u_ref[:, rows] + column(negm_ref)), 0.0)
        p = qk * dmat
        num = jnp.dot(p.astype(BF16), v_ref[rows, :], preferred_element_type=F32)
        num = num + jnp.concatenate([wint, wint], axis=1) * inter
        qn = jnp.sum(p, axis=-1, keepdims=True) + wint * qn_inter
        den = jnp.maximum(jnp.abs(qn), column(floor_ref))
        r = 1.0 / den
        scale = r * lax.rsqrt(jnp.mean(num * num, axis=-1, keepdims=True) * r * r + EPS)
        y = num * jnp.concatenate([scale, scale], axis=1) * hn_ref[...]
        o_ref[rows, :] = (og_ref[rows, :].astype(F32) * y).astype(BF16)

        dec = dec_ref[:, rows]
        cmat = jnp.concatenate([dec, dec], axis=1) * cmat + inc
        nvec = dec * nvec + ksum


def _mlstm_core(q, kT, v, og, h_norm, negm, winter, floor, u, kw, dec):
    b, s, _ = q.shape
    dk, dv = MLSTM_QK_DIM, MLSTM_V_DIM
    d = MLSTM_HEADS * dv
    rowspec = pl.BlockSpec((None, None, 1, s), lambda i, j: (i, j, 0, 0))
    per_head_row = lambda a: a.reshape(b, MLSTM_HEADS, 1, s)
    return pl.pallas_call(
        _mlstm_core_body,
        out_shape=jax.ShapeDtypeStruct((b, s, d), BF16),
        grid=(b, MLSTM_HEADS),
        in_specs=[
            pl.BlockSpec((None, s, dk), lambda i, j: (i, 0, j)),
            pl.BlockSpec((None, dk, s), lambda i, j: (i, j, 0)),
            pl.BlockSpec((None, s, dv), lambda i, j: (i, 0, j)),
            pl.BlockSpec((None, s, dv), lambda i, j: (i, 0, j)),
            pl.BlockSpec((1, dv), lambda i, j: (0, j)),
            rowspec, rowspec, rowspec, rowspec, rowspec, rowspec,
        ],
        out_specs=pl.BlockSpec((None, s, dv), lambda i, j: (i, 0, j)),
        compiler_params=_params("parallel", "parallel"),
        name="mlstm_core",
    )(q, kT, v, og, h_norm.reshape(1, d), *(per_head_row(a) for a in (negm, winter, floor, u, kw, dec)))


def _mlstm_layer(x3d, mix_norm, w_in, b_gates, h_norm, w_out, cast=()):
    b, s, d = x3d.shape
    nh = MLSTM_HEADS
    nqk = nh * MLSTM_QK_DIM
    w_in_bf16 = w_in.astype(BF16)
    wg = w_in[:, 2 * nqk + 2 * d:]
    pad = lambda w: jnp.pad(w, ((0, 0), (0, LANES - nh)))
    w_gates = jnp.concatenate([pad(wg[:, :nh]), pad(wg[:, nh:])], axis=1).astype(BF16)
    bg = b_gates[d:].astype(F32).reshape(1, 2 * nh)
    b_gates_row = jnp.concatenate([pad(bg[:, :nh]), pad(bg[:, nh:])], axis=1)
    bo = b_gates[:d].astype(F32).reshape(1, d)
    (q, kT, v, og, gi, gf), (w_out_bf16, *casted) = _mlstm_proj(
        x3d, mix_norm, w_in_bf16, w_gates, bo, b_gates_row, (w_out,) + tuple(cast))
    negm, winter, floor, u, kw, dec = _mlstm_gates(gi, gf)
    hgated = _mlstm_core(q, kT, v, og, h_norm.astype(F32), negm, winter, floor, u, kw, dec)
    out = _proj_res(hgated.reshape(b * s, d), w_out_bf16, x3d.reshape(b * s, d))
    return out.reshape(b, s, d), casted


def kernel(x, l0_mix_norm, l0_attn_w_qkv, l0_attn_q_norm, l0_attn_k_norm, l0_attn_sinks, l0_attn_w_o, l0_mlp_norm, l0_mlp_w1, l0_mlp_w2, l1_mix_norm, l1_conv_w_in, l1_conv_b_in, l1_conv_dw, l1_conv_dw_b, l1_conv_ln_g, l1_conv_ln_b, l1_conv_w_out, l1_conv_b_out, l1_mlp_norm, l1_mlp_w1, l1_mlp_w2, l2_mix_norm, l2_mlstm_w_in, l2_mlstm_b_gates, l2_mlstm_h_norm, l2_mlstm_w_out, l2_mlp_norm, l2_mlp_w1, l2_mlp_w2, l3_mix_norm, l3_attn_w_qkv, l3_attn_q_norm, l3_attn_k_norm, l3_attn_sinks, l3_attn_w_o, l3_mlp_norm, l3_mlp_w1, l3_mlp_w2):
    b, s, d = x.shape

    def mlp(x3d, norm, w1, w2):
        return _mlp(x3d.reshape(b * s, d), norm.astype(F32), w1, w2).reshape(b, s, d)

    x, (w1, w2, conv_w_in, conv_w_out) = _attention_layer(
        x, l0_mix_norm, l0_attn_w_qkv, l0_attn_q_norm, l0_attn_k_norm, l0_attn_sinks, l0_attn_w_o,
        cast=(l0_mlp_w1, l0_mlp_w2, l1_conv_w_in, l1_conv_w_out))
    x = mlp(x, l0_mlp_norm, w1, w2)
    x, (w1, w2) = _conv_layer(
        x, l1_mix_norm, conv_w_in, l1_conv_b_in, l1_conv_dw, l1_conv_dw_b, l1_conv_ln_g, l1_conv_ln_b, conv_w_out,
        l1_conv_b_out, cast=(l1_mlp_w1, l1_mlp_w2))
    x = mlp(x, l1_mlp_norm, w1, w2)
    x, (w1, w2) = _mlstm_layer(x, l2_mix_norm, l2_mlstm_w_in, l2_mlstm_b_gates, l2_mlstm_h_norm, l2_mlstm_w_out,
                               cast=(l2_mlp_w1, l2_mlp_w2))
    x = mlp(x, l2_mlp_norm, w1, w2)
    x, (w1, w2) = _attention_layer(
        x, l3_mix_norm, l3_attn_w_qkv, l3_attn_q_norm, l3_attn_k_norm, l3_attn_sinks, l3_attn_w_o,
        cast=(l3_mlp_w1, l3_mlp_w2))
    x = mlp(x, l3_mlp_norm, w1, w2)
    return x
```
